```python
import math
import jax
import jax.numpy as jnp
from jax import lax
import numpy as np

D_MODEL = 2048
BATCH = 8
SEQ = 2048
DEPTH = 2
DEC_BATCH = 128
DEC_SEQ = 4
PAST_LEN = 2048
PAGE_SIZE = 128

HEAD_DIM = 128
NSA_HEADS = D_MODEL // (2 * HEAD_DIM)
NSA_KV = max(1, NSA_HEADS // 4)
NSA_REP = NSA_HEADS // NSA_KV
FOX_HEADS = D_MODEL // (2 * HEAD_DIM)
NSA_WIDTH = NSA_HEADS * HEAD_DIM
FOX_WIDTH = FOX_HEADS * HEAD_DIM
CMP_LEN = 32
CMP_STRIDE = 16
CMP_HID = 256
SEL_BLOCK = 64
SEL_TOPN = 16
WINDOW = 512
Q_BLOCK = 128
REL_BUCKETS = 32
REL_MAX_DIST = 128
PEER_HEADS = 8
PEER_NKEYS = 128
PEER_EXPERTS = PEER_NKEYS * PEER_NKEYS
PEER_DK = 256
PEER_TOPK = 16
PEER_BLOCK = 512
N_IN = NSA_WIDTH + 6 * NSA_KV * HEAD_DIM + 3 * NSA_HEADS + 3 * FOX_WIDTH + FOX_HEADS + 2 * D_MODEL
DEEPNORM_ALPHA = (2 * DEPTH) ** 0.25
DEEPNORM_BETA = (8 * DEPTH) ** -0.25
LN_EPS = 1e-5
NEG_INF = -1e9
FORCE_SCORE = 1e9

kernel_name = 'hybrid_nsa_fox_peer_step'


def layer_norm(x, g, b):
    xf = x.astype(jnp.float32)
    mu = jnp.mean(xf, -1, keepdims=True)
    var = jnp.mean(jnp.square(xf - mu), -1, keepdims=True)
    y = (xf - mu) * lax.rsqrt(var + LN_EPS) * g.astype(jnp.float32) + b.astype(jnp.float32)
    return y.astype(x.dtype)


def ada_mod(c, w, b):
    return (jax.nn.silu(c) @ w + b).reshape(c.shape[0], 6, D_MODEL)


def modulate(x, shift, scale):
    return x * (1 + scale[:, None]) + shift[:, None]


def masked_softmax(s, mask):
    s = jnp.where(mask, s, NEG_INF)
    p = jnp.exp(s - jnp.max(s, -1, keepdims=True)) * mask
    return p / jnp.maximum(jnp.sum(p, -1, keepdims=True), 1e-30)


def rel_bucket(dist):
    n = jnp.maximum(dist, 0)
    exact = REL_BUCKETS // 2
    scaled = jnp.log(jnp.maximum(n, 1).astype(jnp.float32) / exact) / math.log(REL_MAX_DIST / exact)
    large = jnp.minimum(exact + (scaled * (REL_BUCKETS - exact)).astype(jnp.int32), REL_BUCKETS - 1)
    return jnp.where(n < exact, n, large)


def mixer_inputs(h, w_in, b_forget):
    n, t = h.shape[:2]
    sizes = (NSA_WIDTH, 6 * NSA_KV * HEAD_DIM, 3 * NSA_HEADS, 3 * FOX_WIDTH, FOX_HEADS, 2 * D_MODEL)
    offs = [int(o) for o in np.cumsum(sizes)[:-1]]
    q_a, kv_a, g_a, qkv_b, f_b, g_m = jnp.split(h @ w_in, offs, axis=-1)
    q_a = q_a.reshape(n, t, NSA_KV, NSA_REP, HEAD_DIM)
    kv_a = kv_a.reshape(n, t, 6, NSA_KV, HEAD_DIM)
    g_a = jax.nn.sigmoid(g_a.reshape(n, t, NSA_KV, NSA_REP, 3))
    qkv_b = qkv_b.reshape(n, t, 3, FOX_HEADS, HEAD_DIM)
    logf = jax.nn.log_sigmoid(f_b.astype(jnp.float32) + b_forget.astype(jnp.float32))
    g_m = jax.nn.sigmoid(g_m.reshape(n, t, 2, D_MODEL))
    return q_a, kv_a, g_a, qkv_b[:, :, 0], qkv_b[:, :, 1], qkv_b[:, :, 2], logf, g_m


def gather_pages(cache, page_table):
    rows = cache[page_table]
    return rows.reshape((page_table.shape[0], -1) + cache.shape[2:])


def gqa_attend(q, k, v, dist, mask, rel_table):
    t, kk = dist.shape
    s = jnp.einsum('ntgrd,nkgd->ngrtk', q, k).astype(jnp.float32) * HEAD_DIM ** -0.5
    bias = rel_table[rel_bucket(dist)].reshape(t, kk, NSA_KV, NSA_REP).transpose(2, 3, 0, 1)
    p = masked_softmax(s + bias, mask)
    return jnp.einsum('ngrtk,nkgd->ntgrd', p.astype(v.dtype), v), p


def nsa_compress(kv, pos, w1, w2):
    n, l = kv.shape[:2]
    nch = l // CMP_STRIDE
    ch = kv[:, :nch * CMP_STRIDE].reshape(n, nch, CMP_STRIDE, NSA_KV, HEAD_DIM)
    w1r = w1.reshape(CMP_LEN, HEAD_DIM, CMP_HID)
    first = jnp.einsum('ncsgd,sdh->ncgh', ch, w1r[:CMP_STRIDE])
    second = jnp.einsum('ncsgd,sdh->ncgh', ch, w1r[CMP_STRIDE:])
    pos_term = jnp.einsum('sd,sdh->h', pos, w1r)
    hid = jax.nn.gelu(first[:, :-1] + second[:, 1:] + pos_term)
    return hid @ w2


def overlap_matrix(nb, ns):
    i = jnp.arange(nb)[:, None]
    j = jnp.arange(ns)[None, :]
    rs = SEL_BLOCK // CMP_STRIDE
    rc = CMP_LEN // CMP_STRIDE
    m = jnp.zeros((nb, ns), jnp.float32)
    for a in range(rs):
        for b in range(rc):
            m = m + (i == j * rs + a + b - (rc - 1)).astype(jnp.float32)
    return m


def nsa_selected(q, k, v, idx, tq, rel_table, qb):
    n, t = q.shape[:2]
    l = k.shape[1]
    ns = -(-l // SEL_BLOCK)
    n_sel = idx.shape[-1]
    pad = ((0, 0), (0, ns * SEL_BLOCK - l), (0, 0), (0, 0))
    kb = jnp.pad(k, pad).reshape(n, ns, SEL_BLOCK, NSA_KV, HEAD_DIM).transpose(0, 3, 1, 2, 4)
    vb = jnp.pad(v, pad).reshape(n, ns, SEL_BLOCK, NSA_KV, HEAD_DIM).transpose(0, 3, 1, 2, 4)
    nq = t // qb
    rel_g = rel_table.reshape(REL_BUCKETS, NSA_KV, NSA_REP)
    g_ix = jnp.arange(NSA_KV)[None, :, None]
    offs = jnp.arange(SEL_BLOCK)

    def one(item):
        qi, ii, ti, bi = item
        k_sel = kb[bi][g_ix, ii]
        v_sel = vb[bi][g_ix, ii]
        dist = ti[:, None, None, None] - (ii[..., None] * SEL_BLOCK + offs)
        bias = jnp.moveaxis(rel_g[rel_bucket(dist), g_ix[..., None]], -1, 2)
        s = jnp.einsum('qgrd,qgjsd->qgrjs', qi, k_sel).astype(jnp.float32) * HEAD_DIM ** -0.5 + bias
        p = masked_softmax(s.reshape(qb, NSA_KV, NSA_REP, -1), (dist >= 0).reshape(qb, NSA_KV, 1, -1))
        return jnp.einsum('qgrjs,qgjsd->qgrd', p.reshape(s.shape).astype(v_sel.dtype), v_sel)

    items = (q.reshape(n * nq, qb, NSA_KV, NSA_REP, HEAD_DIM),
             idx.reshape(n * nq, qb, NSA_KV, n_sel),
             jnp.tile(tq.reshape(nq, qb), (n, 1)),
             jnp.repeat(jnp.arange(n), nq))
    return lax.map(one, items).reshape(n, t, NSA_KV, NSA_REP, HEAD_DIM)


def nsa_cmp_sel(q, k_cmp, v_cmp, k_sel, v_sel, tq, pos, w1, w2, rel_table, qb):
    kc = nsa_compress(k_cmp, pos[0], w1[0], w2[0])
    vc = nsa_compress(v_cmp, pos[1], w1[1], w2[1])
    nb = kc.shape[1]
    end = jnp.arange(nb) * CMP_STRIDE + (CMP_LEN - 1)
    dist = tq[:, None] - end[None, :]
    o_cmp, p_cmp = gqa_attend(q, kc, vc, dist, dist >= 0, rel_table)
    ns = -(-k_sel.shape[1] // SEL_BLOCK)
    imp = jnp.einsum('ngrtb,bj->ntgj', p_cmp, overlap_matrix(nb, ns))
    blk = jnp.arange(ns)[None, :]
    cur = (tq // SEL_BLOCK)[:, None]
    valid = blk * SEL_BLOCK <= tq[:, None]
    forced = (blk == 0) | (blk == cur) | (blk == cur - 1)
    score = jnp.where(forced[None, :, None], FORCE_SCORE, jnp.where(valid[None, :, None], imp, NEG_INF))
    _, idx = lax.top_k(score, min(SEL_TOPN, ns))
    o_sel = nsa_selected(q, k_sel, v_sel, idx, tq, rel_table, qb)
    return o_cmp, o_sel


def nsa_window_banded(q, k, v, rel_table):
    b, s = q.shape[:2]
    nq = s // Q_BLOCK
    nw = WINDOW // Q_BLOCK
    nk = (nw + 1) * Q_BLOCK
    pad = ((0, 0), (WINDOW, 0), (0, 0), (0, 0))
    kp = jnp.pad(k, pad).reshape(b, nq + nw, Q_BLOCK, NSA_KV, HEAD_DIM)
    vp = jnp.pad(v, pad).reshape(b, nq + nw, Q_BLOCK, NSA_KV, HEAD_DIM)
    kband = jnp.concatenate([kp[:, j:j + nq] for j in range(nw + 1)], axis=2)
    vband = jnp.concatenate([vp[:, j:j + nq] for j in range(nw + 1)], axis=2)
    qb = q.reshape(b, nq, Q_BLOCK, NSA_KV, NSA_REP, HEAD_DIM)
    sc = jnp.einsum('bnqgrd,bnkgd->bngrqk', qb, kband).astype(jnp.float32) * HEAD_DIM ** -0.5
    tq = jnp.arange(nq)[:, None] * Q_BLOCK + jnp.arange(Q_BLOCK)[None, :]
    tk = jnp.arange(nq)[:, None] * Q_BLOCK - WINDOW + jnp.arange(nk)[None, :]
    dist = tq[:, :, None] - tk[:, None, :]
    mask = (dist >= 0) & (dist <= WINDOW) & (tk[:, None, :] >= 0)
    bias = rel_table[rel_bucket(dist)].reshape(nq, Q_BLOCK, nk, NSA_KV, NSA_REP).transpose(0, 3, 4, 1, 2)
    p = masked_softmax(sc + bias, mask[:, None, None])
    o = jnp.einsum('bngrqk,bnkgd->bnqgrd', p.astype(v.dtype), vband)
    return o.reshape(b, s, NSA_KV, NSA_REP, HEAD_DIM)


def nsa_gate(g_a, o_cmp, o_sel, o_win):
    return g_a[..., 0, None] * o_cmp + g_a[..., 1, None] * o_sel + g_a[..., 2, None] * o_win


def fox_attend(q, k, v, f_q, f_k, mask):
    s = jnp.einsum('nthd,nkhd->nhtk', q, k).astype(jnp.float32) * HEAD_DIM ** -0.5
    s = s + jnp.swapaxes(f_q, 1, 2)[..., :, None] - jnp.swapaxes(f_k, 1, 2)[..., None, :]
    p = masked_softmax(s, mask)
    return jnp.einsum('nhtk,nkhd->nthd', p.astype(v.dtype), v)


def fox_blocked(q, k, v, f):
    n, s = q.shape[:2]
    nq = s // Q_BLOCK
    tk = jnp.arange(s)

    def one(i):
        start = i * Q_BLOCK
        qi = lax.dynamic_slice_in_dim(q, start, Q_BLOCK, axis=1)
        fi = lax.dynamic_slice_in_dim(f, start, Q_BLOCK, axis=1)
        tq = start + jnp.arange(Q_BLOCK)
        return fox_attend(qi, k, v, fi, f, tk[None, :] <= tq[:, None])

    o = lax.map(one, jnp.arange(nq))
    return jnp.moveaxis(o, 0, 1).reshape(n, s, FOX_HEADS, HEAD_DIM)


def merge_branches(o_a, o_b, g_m, wa, wb, wo):
    n, t = o_b.shape[:2]
    y_a = o_a.reshape(n, t, NSA_WIDTH) @ wa
    y_b = o_b.reshape(n, t, FOX_WIDTH) @ wb
    return (g_m[:, :, 0] * y_a + g_m[:, :, 1] * y_b) @ wo


def peer_ffn(h, wq, subkeys, u, v):
    n, t, d = h.shape
    ntok = n * t
    nblk = -(-ntok // PEER_BLOCK)
    tok = jnp.pad(h.reshape(ntok, d), ((0, nblk * PEER_BLOCK - ntok), (0, 0))).reshape(nblk, PEER_BLOCK, d)
    rows = jnp.arange(PEER_BLOCK)[:, None]

    def one(xb):
        q = (xb @ wq).reshape(PEER_BLOCK, PEER_HEADS, 2, PEER_DK // 2)
        sc = jnp.einsum('thcd,hckd->thck', q, subkeys).astype(jnp.float32)
        s1, i1 = lax.top_k(sc[:, :, 0], PEER_TOPK)
        s2, i2 = lax.top_k(sc[:, :, 1], PEER_TOPK)
        cand = (s1[..., :, None] + s2[..., None, :]).reshape(PEER_BLOCK, PEER_HEADS, PEER_TOPK * PEER_TOPK)
        top_s, top_c = lax.top_k(cand, PEER_TOPK)
        eid = (jnp.take_along_axis(i1, top_c // PEER_TOPK, axis=-1) * PEER_NKEYS
               + jnp.take_along_axis(i2, top_c % PEER_TOPK, axis=-1)).reshape(PEER_BLOCK, -1)
        gate = jax.nn.softmax(top_s, axis=-1).reshape(PEER_BLOCK, -1)
        pre = jnp.take_along_axis(jnp.einsum('td,ed->te', xb, u), eid, axis=-1).astype(jnp.float32)
        w = (gate * jax.nn.gelu(pre)).astype(xb.dtype)
        dense = jnp.zeros((PEER_BLOCK, PEER_EXPERTS), xb.dtype).at[rows, eid].add(w)
        return dense @ v

    y = lax.map(one, tok).reshape(nblk * PEER_BLOCK, d)[:ntok]
    return y.reshape(n, t, d)


def setup_inputs(seed: int = 0) -> dict:
    key = jax.random.key(seed)
    ks = jax.random.split(key, 32)
    f32 = jnp.float32
    n_pages = PAST_LEN // PAGE_SIZE
    n_used = DEC_BATCH * n_pages
    n_pool = n_used + max(1, n_used // 4)
    win_buf = min(WINDOW, PAST_LEN)

    def nrm(k, shape, scale=1.0):
        return jax.random.normal(k, shape, f32) * scale

    page_table = jax.random.permutation(ks[9], n_pool)[:n_used].reshape(DEC_BATCH, n_pages).astype(jnp.int32)
    return {
        'x_prompt': nrm(ks[0], (BATCH, SEQ, D_MODEL)),
        'x_sample': nrm(ks[1], (DEC_BATCH, DEC_SEQ, D_MODEL)),
        'cache_nsa_k': nrm(ks[2], (DEPTH, n_pool, PAGE_SIZE, 2, NSA_KV, HEAD_DIM)),
        'cache_nsa_v': nrm(ks[3], (DEPTH, n_pool, PAGE_SIZE, 2, NSA_KV, HEAD_DIM)),
        'cache_fox_k': nrm(ks[4], (DEPTH, n_pool, PAGE_SIZE, FOX_HEADS, HEAD_DIM)),
        'cache_fox_v': nrm(ks[5], (DEPTH, n_pool, PAGE_SIZE, FOX_HEADS, HEAD_DIM)),
        'cache_fox_logf': jax.nn.log_sigmoid(nrm(ks[6], (DEPTH, n_pool, PAGE_SIZE, FOX_HEADS)) + 4.5),
        'state_nsa_win_k': nrm(ks[7], (DEPTH, DEC_BATCH, win_buf, NSA_KV, HEAD_DIM)),
        'state_nsa_win_v': nrm(ks[8], (DEPTH, DEC_BATCH, win_buf, NSA_KV, HEAD_DIM)),
        'page_table': page_table,
        'c_prompt': nrm(ks[10], (BATCH, D_MODEL)),
        'c_sample': nrm(ks[11], (DEC_BATCH, D_MODEL)),
        'w_ada': nrm(ks[12], (DEPTH, D_MODEL, 6 * D_MODEL), 0.5 * D_MODEL ** -0.5),
        'b_ada': nrm(ks[13], (DEPTH, 6 * D_MODEL), 0.01),
        'w_in': nrm(ks[14], (DEPTH, D_MODEL, N_IN), D_MODEL ** -0.5),
        'b_forget': jax.random.uniform(ks[15], (DEPTH, FOX_HEADS), f32, 3.0, 6.0),
        'cmp_pos': nrm(ks[16], (DEPTH, 2, CMP_LEN, HEAD_DIM), 0.1),
        'cmp_w1': nrm(ks[17], (DEPTH, 2, CMP_LEN * HEAD_DIM, CMP_HID), (CMP_LEN * HEAD_DIM) ** -0.5),
        'cmp_w2': nrm(ks[18], (DEPTH, 2, CMP_HID, HEAD_DIM), CMP_HID ** -0.5),
        'rel_table': nrm(ks[19], (REL_BUCKETS, NSA_HEADS), 0.5),
        'w_branch_a': nrm(ks[20], (DEPTH, NSA_WIDTH, D_MODEL), NSA_WIDTH ** -0.5),
        'w_branch_b': nrm(ks[21], (DEPTH, FOX_WIDTH, D_MODEL), FOX_WIDTH ** -0.5),
        'w_out': nrm(ks[22], (DEPTH, D_MODEL, D_MODEL), DEEPNORM_BETA * D_MODEL ** -0.5),
        'ln_g': 1.0 + nrm(ks[23], (DEPTH, 2, D_MODEL), 0.02),
        'ln_b': nrm(ks[24], (DEPTH, 2, D_MODEL), 0.02),
        'peer_wq': nrm(ks[25], (DEPTH, D_MODEL, PEER_HEADS * PEER_DK), D_MODEL ** -0.5),
        'peer_subkeys': nrm(ks[26], (DEPTH, PEER_HEADS, 2, PEER_NKEYS, PEER_DK // 2), (PEER_DK // 2) ** -0.5),
        'peer_u': nrm(ks[27], (DEPTH, PEER_EXPERTS, D_MODEL), D_MODEL ** -0.5),
        'peer_v': nrm(ks[28], (DEPTH, PEER_EXPERTS, D_MODEL), DEEPNORM_BETA),
    }


def reference(x_prompt, x_sample, cache_nsa_k, cache_nsa_v, cache_fox_k, cache_fox_v, cache_fox_logf,
              state_nsa_win_k, state_nsa_win_v, page_table, c_prompt, c_sample, w_ada, b_ada, w_in,
              b_forget, cmp_pos, cmp_w1, cmp_w2, rel_table, w_branch_a, w_branch_b, w_out, ln_g, ln_b,
              peer_wq, peer_subkeys, peer_u, peer_v):
    seq = x_prompt.shape[1]
    dec_seq = x_sample.shape[1]
    past = page_table.shape[1] * cache_nsa_k.shape[2]
    win_buf = state_nsa_win_k.shape[2]
    keep_p = min(WINDOW, seq)
    tq_p = jnp.arange(seq)
    tq_s = past + jnp.arange(dec_seq)
    tk_win = past - win_buf + jnp.arange(win_buf + dec_seq)
    d_win = tq_s[:, None] - tk_win[None, :]
    m_win = (d_win >= 0) & (d_win <= WINDOW)
    m_fox = jnp.arange(past + dec_seq)[None, :] <= tq_s[:, None]
    names = ('nsa_k', 'nsa_v', 'fox_k', 'fox_v', 'fox_logf', 'win_k', 'win_v')
    new_p = {nm: [] for nm in names}
    new_s = {nm: [] for nm in names}
    xp, xs = x_prompt, x_sample
    for l in range(DEPTH):
        mod = ada_mod(c_prompt, w_ada[l], b_ada[l])
        h = modulate(xp, mod[:, 0], mod[:, 1])
        q_a, kv_a, g_a, q_b, k_b, v_b, logf, g_m = mixer_inputs(h, w_in[l], b_forget[l])
        o_cmp, o_sel = nsa_cmp_sel(q_a, kv_a[:, :, 0], kv_a[:, :, 1], kv_a[:, :, 2], kv_a[:, :, 3], tq_p,
                                   cmp_pos[l], cmp_w1[l], cmp_w2[l], rel_table, Q_BLOCK)
        o_win = nsa_window_banded(q_a, kv_a[:, :, 4], kv_a[:, :, 5], rel_table)
        o_b = fox_blocked(q_b, k_b, v_b, jnp.cumsum(logf, axis=1))
        mix = merge_branches(nsa_gate(g_a, o_cmp, o_sel, o_win), o_b, g_m, w_branch_a[l], w_branch_b[l], w_out[l])
        xp = layer_norm(DEEPNORM_ALPHA * xp + mod[:, 2, None] * mix, ln_g[l, 0], ln_b[l, 0])
        h = modulate(xp, mod[:, 3], mod[:, 4])
        ffn = peer_ffn(h, peer_wq[l], peer_subkeys[l], peer_u[l], peer_v[l])
        xp = layer_norm(DEEPNORM_ALPHA * xp + mod[:, 5, None] * ffn, ln_g[l, 1], ln_b[l, 1])
        new_p['nsa_k'].append(kv_a[:, :, 0:4:2])
        new_p['nsa_v'].append(kv_a[:, :, 1:4:2])
        new_p['fox_k'].append(k_b)
        new_p['fox_v'].append(v_b)
        new_p['fox_logf'].append(logf)
        new_p['win_k'].append(kv_a[:, seq - keep_p:, 4])
        new_p['win_v'].append(kv_a[:, seq - keep_p:, 5])

        mod = ada_mod(c_sample, w_ada[l], b_ada[l])
        h = modulate(xs, mod[:, 0], mod[:, 1])
        q_a, kv_a, g_a, q_b, k_b, v_b, logf, g_m = mixer_inputs(h, w_in[l], b_forget[l])
        full_k = jnp.concatenate([gather_pages(cache_nsa_k[l], page_table), kv_a[:, :, 0:4:2]], axis=1)
        full_v = jnp.concatenate([gather_pages(cache_nsa_v[l], page_table), kv_a[:, :, 1:4:2]], axis=1)
        o_cmp, o_sel = nsa_cmp_sel(q_a, full_k[:, :, 0], full_v[:, :, 0], full_k[:, :, 1], full_v[:, :, 1], tq_s,
                                   cmp_pos[l], cmp_w1[l], cmp_w2[l], rel_table, dec_seq)
        win_k = jnp.concatenate([state_nsa_win_k[l], kv_a[:, :, 4]], axis=1)
        win_v = jnp.concatenate([state_nsa_win_v[l], kv_a[:, :, 5]], axis=1)
        o_win, _ = gqa_attend(q_a, win_k, win_v, d_win, m_win, rel_table)
        fk = jnp.concatenate([gather_pages(cache_fox_k[l], page_table), k_b], axis=1)
        fv = jnp.concatenate([gather_pages(cache_fox_v[l], page_table), v_b], axis=1)
        f_all = jnp.cumsum(jnp.concatenate([gather_pages(cache_fox_logf[l], page_table).astype(jnp.float32), logf], axis=1), axis=1)
        o_b = fox_attend(q_b, fk, fv, f_all[:, past:], f_all, m_fox)
        mix = merge_branches(nsa_gate(g_a, o_cmp, o_sel, o_win), o_b, g_m, w_branch_a[l], w_branch_b[l], w_out[l])
        xs = layer_norm(DEEPNORM_ALPHA * xs + mod[:, 2, None] * mix, ln_g[l, 0], ln_b[l, 0])
        h = modulate(xs, mod[:, 3], mod[:, 4])
        ffn = peer_ffn(h, peer_wq[l], peer_subkeys[l], peer_u[l], peer_v[l])
        xs = layer_norm(DEEPNORM_ALPHA * xs + mod[:, 5, None] * ffn, ln_g[l, 1], ln_b[l, 1])
        new_s['nsa_k'].append(kv_a[:, :, 0:4:2])
        new_s['nsa_v'].append(kv_a[:, :, 1:4:2])
        new_s['fox_k'].append(k_b)
        new_s['fox_v'].append(v_b)
        new_s['fox_logf'].append(logf)
        new_s['win_k'].append(win_k[:, dec_seq:])
        new_s['win_v'].append(win_v[:, dec_seq:])

    return (xp, xs,
            jnp.stack(new_p['nsa_k']), jnp.stack(new_p['nsa_v']), jnp.stack(new_p['fox_k']),
            jnp.stack(new_p['fox_v']), jnp.stack(new_p['fox_logf']), jnp.stack(new_p['win_k']),
            jnp.stack(new_p['win_v']),
            jnp.stack(new_s['nsa_k']), jnp.stack(new_s['nsa_v']), jnp.stack(new_s['fox_k']),
            jnp.stack(new_s['fox_v']), jnp.stack(new_s['fox_logf']), jnp.stack(new_s['win_k']),
            jnp.stack(new_s['win_v']))
```

```python
import functools
import math

import numpy as np
import jax
import jax.numpy as jnp
from jax import lax
from jax.experimental import pallas as pl
from jax.experimental.pallas import tpu as pltpu

F32 = jnp.float32
BF = jnp.bfloat16

D_MODEL = 2048
DEPTH = 2
HEAD_DIM = 128
NSA_HEADS = D_MODEL // (2 * HEAD_DIM)
NSA_KV = max(1, NSA_HEADS // 4)
NSA_REP = NSA_HEADS // NSA_KV
FOX_HEADS = D_MODEL // (2 * HEAD_DIM)
NSA_WIDTH = NSA_HEADS * HEAD_DIM
FOX_WIDTH = FOX_HEADS * HEAD_DIM
CMP_LEN = 32
CMP_STRIDE = 16
CMP_HID = 256
SEL_BLOCK = 64
SEL_TOPN = 16
WINDOW = 512
REL_BUCKETS = 32
REL_MAX_DIST = 128
PEER_HEADS = 8
PEER_NKEYS = 128
PEER_EXPERTS = PEER_NKEYS * PEER_NKEYS
PEER_DK = 256
PEER_TOPK = 16
DEEPNORM_ALPHA = (2 * DEPTH) ** 0.25
LN_EPS = 1e-5
NEG_INF = -1e9
FORCE_SCORE = 1e9
SCALE = HEAD_DIM ** -0.5

LANES = 128
SUB = 8
VMEM_MB = 1024 * 1024

QA0 = 0
KV0 = QA0 + NSA_WIDTH
QB0 = KV0 + 6 * NSA_KV * HEAD_DIM
GM0 = QB0 + 3 * FOX_WIDTH
GF0 = GM0 + 2 * D_MODEL
PROJ_COLS = 10240
FB_LANE = 3 * NSA_HEADS
SAMPLE_ROWS = 8
PEER_TOK = 512
PEER_ETILE = 512
FLASH_T = 256


def _nt(a, b):
    return lax.dot_general(a, b, (((1,), (1,)), ((), ())), preferred_element_type=F32)


def _tn(a, b):
    return lax.dot_general(a, b, (((0,), (0,)), ((), ())), preferred_element_type=F32)


def _dot(a, b):
    return jnp.dot(a, b, preferred_element_type=F32)


def _split3(x):
    hi = x.astype(BF)
    r1 = x - hi.astype(F32)
    mid = r1.astype(BF)
    lo = (r1 - mid.astype(F32)).astype(BF)
    return hi, mid, lo


def _params(sem, mb):
    return pltpu.CompilerParams(dimension_semantics=sem, vmem_limit_bytes=mb * VMEM_MB)


def _lane_col(x, idx):
    lane = lax.broadcasted_iota(jnp.int32, x.shape, 1)
    return jnp.sum(jnp.where(lane == idx, x, 0.0), axis=-1, keepdims=True)


def _ada_body(c_ref, w_ref, b_ref, o_ref):
    c = c_ref[...]
    a = (c * jax.nn.sigmoid(c)).astype(BF)
    o_ref[...] = _dot(a, w_ref[...].astype(BF)) + b_ref[...]


def _ada(c, w, b):
    m, d = c.shape
    n = w.shape[1]
    tn = 1536
    return pl.pallas_call(
        _ada_body,
        grid=(n // tn,),
        in_specs=[pl.BlockSpec((m, d), lambda j: (0, 0)),
                  pl.BlockSpec((d, tn), lambda j: (0, j)),
                  pl.BlockSpec((1, tn), lambda j: (0, j))],
        out_specs=pl.BlockSpec((m, tn), lambda j: (0, j)),
        out_shape=jax.ShapeDtypeStruct((m, n), F32),
        compiler_params=_params(("parallel",), 48),
        name="ada",
    )(c, w, b)


def _proj_body(x_ref, sh_ref, sc_ref, w_ref, o_ref, *rest, emit_h):
    if emit_h:
        hb_ref, h_ref = rest
    else:
        (h_ref,) = rest

    @pl.when(pl.program_id(2) == 0)
    def _():
        h = (x_ref[0] * (1.0 + sc_ref[0]) + sh_ref[0]).astype(BF)
        h_ref[...] = h
        if emit_h:
            hb_ref[0] = h

    o_ref[0] = _dot(h_ref[...], w_ref[...])


def _proj(x, shift, scale, w, tm, tn, emit_h=False):
    n, t, d = x.shape
    c = w.shape[1]
    ts = shift.shape[1]
    tms = tm if ts == t else 1
    mod_map = (lambda b, i, j: (b, i, 0)) if ts == t else (lambda b, i, j: (b, 0, 0))
    out_shape = [jax.ShapeDtypeStruct((n, t, c), F32)]
    out_specs = [pl.BlockSpec((1, tm, tn), lambda b, i, j: (b, i, j))]
    if emit_h:
        out_shape.append(jax.ShapeDtypeStruct((n, t, d), BF))
        out_specs.append(pl.BlockSpec((1, tm, d), lambda b, i, j: (b, i, 0)))
    res = pl.pallas_call(
        functools.partial(_proj_body, emit_h=emit_h),
        grid=(n, t // tm, c // tn),
        in_specs=[pl.BlockSpec((1, tm, d), lambda b, i, j: (b, i, 0)),
                  pl.BlockSpec((1, tms, d), mod_map),
                  pl.BlockSpec((1, tms, d), mod_map),
                  pl.BlockSpec((d, tn), lambda b, i, j: (0, j))],
        out_specs=out_specs,
        out_shape=out_shape,
        scratch_shapes=[pltpu.VMEM((tm, d), BF)],
        compiler_params=_params(("parallel", "parallel", "arbitrary"), 56),
        name="proj",
    )(x, shift, scale, w)
    return res if emit_h else res[0]


def _cumsum_rows(x, tri):
    hi, mid, lo = _split3(x)
    return _dot(tri, hi) + _dot(tri, mid) + _dot(tri, lo)


def _gates_body(z_ref, bf_ref, ga_ref, lf_ref, f_ref, *, t, blk):
    z = z_ref[0]
    ga_ref[0] = jax.nn.sigmoid(z)
    lf = jax.nn.log_sigmoid(z + bf_ref[...])
    lf_ref[0] = lf
    r = lax.broadcasted_iota(jnp.int32, (blk, blk), 0)
    c = lax.broadcasted_iota(jnp.int32, (blk, blk), 1)
    tri = (r >= c).astype(BF)
    carry = jnp.zeros((1, LANES), F32)
    for b in range(t // blk):
        fb = _cumsum_rows(lf[b * blk:(b + 1) * blk], tri) + carry
        f_ref[0, b * blk:(b + 1) * blk, :] = fb
        carry = fb[blk - 1:blk, :]


def _gates(z, bfpad):
    n, t, _ = z.shape
    blk = min(256, t)
    sds = jax.ShapeDtypeStruct((n, t, LANES), F32)
    spec = pl.BlockSpec((1, t, LANES), lambda b: (b, 0, 0))
    return pl.pallas_call(
        functools.partial(_gates_body, t=t, blk=blk),
        grid=(n,),
        in_specs=[pl.BlockSpec((1, t, LANES), lambda b: (b, 0, GF0 // LANES)),
                  pl.BlockSpec((1, LANES), lambda b: (0, 0))],
        out_specs=[spec, spec, spec],
        out_shape=[sds, sds, sds],
        compiler_params=_params(("parallel",), 32),
        name="gates",
    )(z, bfpad)


def _compress_core(xc, w1_ref, pos_ref, w2_ref, kind):
    m = xc.shape[0]
    half = CMP_STRIDE * HEAD_DIM
    first = _dot(xc, w1_ref[kind, 0:half, :])
    second = _dot(xc, w1_ref[kind, half:2 * half, :])
    posb = jnp.broadcast_to(pos_ref[kind], (SUB, 2 * half)).astype(BF)
    pos_term = _dot(posb, w1_ref[kind])[0:1]
    hid = jax.nn.gelu(first + pltpu.roll(second, m - 1, 0) + pos_term)
    return _dot(hid.astype(BF), w2_ref[kind])


def _compress_p_body(x_ref, w1_ref, pos_ref, w2_ref, o_ref, xc_ref, *, m):
    kind = pl.program_id(1) // NSA_KV
    for s in range(CMP_STRIDE):
        xc_ref[:, s * HEAD_DIM:(s + 1) * HEAD_DIM] = x_ref[0, pl.ds(s, m, stride=CMP_STRIDE), :].astype(BF)
    o_ref[0, 0] = _compress_core(xc_ref[...], w1_ref, pos_ref, w2_ref, kind)


def _compress_p(z, w1, pos, w2):
    n, t, _ = z.shape
    m = t // CMP_STRIDE
    return pl.pallas_call(
        functools.partial(_compress_p_body, m=m),
        grid=(n, 2 * NSA_KV),
        in_specs=[pl.BlockSpec((1, t, HEAD_DIM), lambda b, a: (b, 0, KV0 // HEAD_DIM + a)),
                  pl.BlockSpec(w1.shape, lambda b, a: (0, 0, 0)),
                  pl.BlockSpec(pos.shape, lambda b, a: (0, 0, 0)),
                  pl.BlockSpec(w2.shape, lambda b, a: (0, 0, 0))],
        out_specs=pl.BlockSpec((1, 1, m, HEAD_DIM), lambda b, a: (b, a, 0, 0)),
        out_shape=jax.ShapeDtypeStruct((n, 2 * NSA_KV, m, HEAD_DIM), F32),
        scratch_shapes=[pltpu.VMEM((m, CMP_STRIDE * HEAD_DIM), BF)],
        compiler_params=_params(("parallel", "parallel"), 32),
        name="compress_p",
    )(z, w1, pos, w2)


def _masked_softmax(s, mask):
    s = jnp.where(mask, s, NEG_INF)
    p = jnp.exp(s - jnp.max(s, axis=-1, keepdims=True)) * mask.astype(F32)
    return p / jnp.maximum(jnp.sum(p, axis=-1, keepdims=True), 1e-30)


def _cmp_sel_math(qs, kc, vc, bias, tpos, nb, ns, ovl):
    tq = qs[0].shape[0]
    lane = lax.broadcasted_iota(jnp.int32, (tq, LANES), 1)
    dist = tpos - (lane * CMP_STRIDE + (CMP_LEN - 1))
    mask = (dist >= 0) & (lane < nb)
    kcb = kc.astype(BF)
    vcb = vc.astype(BF)
    outs = []
    psum = jnp.zeros((tq, LANES), F32)
    for r in range(NSA_REP):
        p = _masked_softmax(_nt(qs[r], kcb) * SCALE + bias[r], mask)
        outs.append(_dot(p.astype(BF), vcb))
        psum = psum + p
    hi, mid, lo = _split3(psum)
    imp = _dot(hi, ovl) + _dot(mid, ovl) + _dot(lo, ovl)
    cur = tpos // SEL_BLOCK
    valid = lane * SEL_BLOCK <= tpos
    forced = (lane == 0) | (lane == cur) | (lane == cur - 1)
    score = jnp.where(forced, FORCE_SCORE, jnp.where(valid, imp, NEG_INF))
    score = jnp.where(lane < ns, score, -3.0e38)
    rank = jnp.zeros((tq, LANES), F32)
    for j in range(ns):
        cj = score[:, j:j + 1]
        ahead = (cj > score) | ((cj == score) & (lane > j))
        rank = rank + ahead.astype(F32)
    sel = (rank < float(min(SEL_TOPN, ns))) & (lane < ns)
    return outs, sel.astype(F32)


def _cmp_p_body(q_ref, kc_ref, vc_ref, bias_ref, ga_ref, ovl_ref, o_ref, sel_ref, *, tq, nb, ns):
    g = pl.program_id(1)
    qi = pl.program_id(2)
    q = q_ref[0]
    qs = [q[:, r * HEAD_DIM:(r + 1) * HEAD_DIM].astype(BF) for r in range(NSA_REP)]
    tpos = qi * tq + lax.broadcasted_iota(jnp.int32, (tq, 1), 0)
    outs, sel = _cmp_sel_math(qs, kc_ref[0, 0], vc_ref[0, 0], bias_ref[...], tpos, nb, ns, ovl_ref[...])
    ga = ga_ref[0]
    for r in range(NSA_REP):
        gate = _lane_col(ga, (g * NSA_REP + r) * 3 + 0)
        o_ref[0, :, r * HEAD_DIM:(r + 1) * HEAD_DIM] = outs[r] * gate
    sel_ref[0, 0] = sel


def _cmp_p(z, kvc, bias_cmp, ga, ovl, nb, ns):
    n, t, _ = z.shape
    tq = min(FLASH_T, t)
    gw = NSA_REP * HEAD_DIM
    m = kvc.shape[2]
    return pl.pallas_call(
        functools.partial(_cmp_p_body, tq=tq, nb=nb, ns=ns),
        grid=(n, NSA_KV, t // tq),
        in_specs=[pl.BlockSpec((1, tq, gw), lambda b, g, i: (b, i, g)),
                  pl.BlockSpec((1, 1, m, HEAD_DIM), lambda b, g, i: (b, g, 0, 0)),
                  pl.BlockSpec((1, 1, m, HEAD_DIM), lambda b, g, i: (b, NSA_KV + g, 0, 0)),
                  pl.BlockSpec((NSA_REP, tq, LANES), lambda b, g, i: (g, i, 0)),
                  pl.BlockSpec((1, tq, LANES), lambda b, g, i: (b, i, 0)),
                  pl.BlockSpec((LANES, LANES), lambda b, g, i: (0, 0))],
        out_specs=[pl.BlockSpec((1, tq, gw), lambda b, g, i: (b, i, g)),
                   pl.BlockSpec((1, 1, tq, LANES), lambda b, g, i: (b, g, i, 0))],
        out_shape=[jax.ShapeDtypeStruct((n, t, NSA_WIDTH), F32),
                   jax.ShapeDtypeStruct((n, NSA_KV, t, LANES), F32)],
        compiler_params=_params(("parallel", "parallel", "parallel"), 32),
        name="cmp_p",
    )(z, kvc, kvc, bias_cmp, ga, ovl)


def _flash_body(*refs, mode, tq, nkt):
    if mode == "sel":
        q_ref, k_ref, v_ref, bias_ref, ga_ref, sel_ref, o_ref, m_ref, l_ref, acc_ref = refs
    elif mode == "win":
        q_ref, k_ref, v_ref, bias_ref, ga_ref, o_ref, m_ref, l_ref, acc_ref = refs
    else:
        q_ref, k_ref, v_ref, fcol_ref, frow_ref, o_ref, m_ref, l_ref, acc_ref = refs
    rep = 1 if mode == "fox" else NSA_REP
    hg = pl.program_id(1)
    qi = pl.program_id(2)
    kt = pl.program_id(3)
    if mode == "win":
        tile = qi - (nkt - 1) + kt
        active = tile >= 0
    else:
        tile = kt
        active = tile <= qi

    @pl.when(kt == 0)
    def _():
        m_ref[...] = jnp.full(m_ref.shape, NEG_INF, F32)
        l_ref[...] = jnp.zeros(l_ref.shape, F32)
        acc_ref[...] = jnp.zeros(acc_ref.shape, F32)

    @pl.when(active)
    def _():
        q = q_ref[0]
        if rep == 1:
            qs = q.astype(BF)
        else:
            qs = jnp.concatenate([q[:, r * HEAD_DIM:(r + 1) * HEAD_DIM] for r in range(rep)], axis=0).astype(BF)
        s = _nt(qs, k_ref[0].astype(BF)) * SCALE
        ri = lax.broadcasted_iota(jnp.int32, (tq, tq), 0)
        ci = lax.broadcasted_iota(jnp.int32, (tq, tq), 1)
        dist = (qi - tile) * tq + ri - ci
        if mode == "sel":
            key_blk = (tile * tq + lax.broadcasted_iota(jnp.int32, (LANES, tq), 1)) // SEL_BLOCK
            expand = (key_blk == lax.broadcasted_iota(jnp.int32, (LANES, tq), 0)).astype(BF)
            hit = _dot(sel_ref[0, 0].astype(BF), expand)
            mask = (dist >= 0) & (hit > 0.5)
        elif mode == "win":
            mask = (dist >= 0) & (dist <= WINDOW)
        else:
            mask = dist >= 0
        if mode == "fox":
            fq = _lane_col(fcol_ref[0], FB_LANE + hg)
            s = s + (fq - frow_ref[0, pl.ds(hg, 1), :])
        else:
            s = s + bias_ref[0].reshape(rep * tq, tq)
            mask = jnp.concatenate([mask] * rep, axis=0)
        s = jnp.where(mask, s, NEG_INF)
        m_old = m_ref[...]
        m_new = jnp.maximum(m_old, jnp.max(s, axis=-1, keepdims=True))
        p = jnp.exp(s - m_new) * mask.astype(F32)
        alpha = jnp.exp(m_old - m_new)
        l_ref[...] = l_ref[...] * alpha + jnp.sum(p, axis=-1, keepdims=True)
        acc_ref[...] = acc_ref[...] * alpha + _dot(p.astype(BF), v_ref[0].astype(BF))
        m_ref[...] = m_new

    @pl.when(kt == nkt - 1)
    def _():
        o = acc_ref[...] / jnp.maximum(l_ref[...], 1e-30)
        if mode == "fox":
            o_ref[0] = o
        else:
            ga = ga_ref[0]
            branch = 1 if mode == "sel" else 2
            for r in range(rep):
                gate = _lane_col(ga, (hg * NSA_REP + r) * 3 + branch)
                o_ref[0, :, r * HEAD_DIM:(r + 1) * HEAD_DIM] = o[r * tq:(r + 1) * tq] * gate


def _flash(mode, z, aux):
    n, t, _ = z.shape
    tq = min(FLASH_T, t)
    nq = t // tq
    if mode == "win":
        nkt = min(WINDOW // tq + 1, nq)
        kmap = lambda i, k: jnp.maximum(i - (nkt - 1) + k, 0)
        dmap = lambda i, k: jnp.clip(nkt - 1 - k, 0, 2)
    else:
        nkt = nq
        kmap = lambda i, k: jnp.minimum(k, i)
        dmap = lambda i, k: jnp.clip(i - k, 0, 2)
    if mode == "fox":
        heads, rep = FOX_HEADS, 1
        qc, kc, vc = QB0 // HEAD_DIM, (QB0 + FOX_WIDTH) // HEAD_DIM, (QB0 + 2 * FOX_WIDTH) // HEAD_DIM
        kstep = 1
    else:
        heads, rep = NSA_KV, NSA_REP
        off = 2 if mode == "sel" else 4
        qc = QA0 // (rep * HEAD_DIM)
        kc = KV0 // HEAD_DIM + off * NSA_KV
        vc = KV0 // HEAD_DIM + (off + 1) * NSA_KV
        kstep = 1
    w = rep * HEAD_DIM
    in_specs = [pl.BlockSpec((1, tq, w), lambda b, h, i, k: (b, i, qc + h)),
                pl.BlockSpec((1, tq, HEAD_DIM), lambda b, h, i, k: (b, kmap(i, k), kc + kstep * h)),
                pl.BlockSpec((1, tq, HEAD_DIM), lambda b, h, i, k: (b, kmap(i, k), vc + kstep * h))]
    if mode == "fox":
        f, ft = aux
        args = (z, z, z, f, ft)
        in_specs += [pl.BlockSpec((1, tq, LANES), lambda b, h, i, k: (b, i, 0)),
                     pl.BlockSpec((1, FOX_HEADS, tq), lambda b, h, i, k: (b, 0, kmap(i, k)))]
    else:
        bias, ga = aux[0], aux[1]
        args = (z, z, z, bias, ga)
        in_specs += [pl.BlockSpec((1, rep, tq, tq), lambda b, h, i, k: (dmap(i, k), h, 0, 0)),
                     pl.BlockSpec((1, tq, LANES), lambda b, h, i, k: (b, i, 0))]
        if mode == "sel":
            args += (aux[2],)
            in_specs += [pl.BlockSpec((1, 1, tq, LANES), lambda b, h, i, k: (b, h, i, 0))]
    return pl.pallas_call(
        functools.partial(_flash_body, mode=mode, tq=tq, nkt=nkt),
        grid=(n, heads, nq, nkt),
        in_specs=in_specs,
        out_specs=pl.BlockSpec((1, tq, w), lambda b, h, i, k: (b, i, h)),
        out_shape=jax.ShapeDtypeStruct((n, t, heads * w), F32),
        scratch_shapes=[pltpu.VMEM((rep * tq, 1), F32), pltpu.VMEM((rep * tq, 1), F32),
                        pltpu.VMEM((rep * tq, HEAD_DIM), F32)],
        compiler_params=_params(("parallel", "parallel", "parallel", "arbitrary"), 32),
        name="flash_" + mode,
    )(*args)


def _merge1_body(oc_ref, os_ref, ow_ref, ob_ref, ga_ref, gb_ref, wa_ref, wb_ref, u_ref, oa_s, ob_s):
    @pl.when(pl.program_id(1) == 0)
    def _():
        oa_s[...] = (oc_ref[...] + os_ref[...] + ow_ref[...]).astype(BF)
        ob_s[...] = ob_ref[...].astype(BF)

    ya = _dot(oa_s[...], wa_ref[...])
    yb = _dot(ob_s[...], wb_ref[...])
    u_ref[...] = (jax.nn.sigmoid(ga_ref[...]) * ya + jax.nn.sigmoid(gb_ref[...]) * yb).astype(BF)


def _merge1(oc, osel, ow, ob, z2d, wa, wb):
    rows = oc.shape[0]
    tm, tn = 512, 512
    ospec = pl.BlockSpec((tm, NSA_WIDTH), lambda i, j: (i, 0))
    return pl.pallas_call(
        _merge1_body,
        grid=(rows // tm, D_MODEL // tn),
        in_specs=[ospec, ospec, ospec, ospec,
                  pl.BlockSpec((tm, tn), lambda i, j: (i, GM0 // tn + j)),
                  pl.BlockSpec((tm, tn), lambda i, j: (i, (GM0 + D_MODEL) // tn + j)),
                  pl.BlockSpec((NSA_WIDTH, tn), lambda i, j: (0, j)),
                  pl.BlockSpec((FOX_WIDTH, tn), lambda i, j: (0, j))],
        out_specs=pl.BlockSpec((tm, tn), lambda i, j: (i, j)),
        out_shape=jax.ShapeDtypeStruct((rows, D_MODEL), BF),
        scratch_shapes=[pltpu.VMEM((tm, NSA_WIDTH), BF), pltpu.VMEM((tm, FOX_WIDTH), BF)],
        compiler_params=_params(("parallel", "arbitrary"), 48),
        name="merge1",
    )(oc, osel, ow, ob, z2d, z2d, wa, wb)


def _layer_norm(y, g, b):
    mu = jnp.mean(y, axis=-1, keepdims=True)
    yc = y - mu
    var = jnp.mean(yc * yc, axis=-1, keepdims=True)
    return yc * lax.rsqrt(var + LN_EPS) * g + b


def _merge2_body(u_ref, wo_ref, x_ref, gt_ref, g_ref, b_ref, o_ref):
    mix = _dot(u_ref[0], wo_ref[...])
    y = DEEPNORM_ALPHA * x_ref[0] + gt_ref[0] * mix
    o_ref[0] = _layer_norm(y, g_ref[...], b_ref[...])


def _merge2(u, wo, x, gate, g, b):
    n, t, d = x.shape
    tm = min(512, t)
    ts = gate.shape[1]
    gspec = (pl.BlockSpec((1, tm, d), lambda bb, i: (bb, i, 0)) if ts == t
             else pl.BlockSpec((1, 1, d), lambda bb, i: (bb, 0, 0)))
    vec = pl.BlockSpec((1, d), lambda bb, i: (0, 0))
    return pl.pallas_call(
        _merge2_body,
        grid=(n, t // tm),
        in_specs=[pl.BlockSpec((1, tm, d), lambda bb, i: (bb, i, 0)),
                  pl.BlockSpec((d, d), lambda bb, i: (0, 0)),
                  pl.BlockSpec((1, tm, d), lambda bb, i: (bb, i, 0)),
                  gspec, vec, vec],
        out_specs=pl.BlockSpec((1, tm, d), lambda bb, i: (bb, i, 0)),
        out_shape=jax.ShapeDtypeStruct((n, t, d), F32),
        compiler_params=_params(("parallel", "parallel"), 56),
        name="merge2",
    )(u, wo, x, gate, g, b)


def _top_desc(x, k):
    n, t = x.shape
    ri = lax.broadcasted_iota(jnp.int32, (n, t), 0)
    ro = lax.broadcasted_iota(jnp.int32, (k, t), 0)

    def step(it, carry):
        xs, out = carry
        m = jnp.max(xs, axis=0, keepdims=True)
        first = jnp.min(jnp.where(xs == m, ri, n), axis=0, keepdims=True)
        xs = jnp.where(ri == first, -jnp.inf, xs)
        out = jnp.where(ro == it, m, out)
        return xs, out

    _, out = lax.fori_loop(0, k, step, (x, jnp.zeros((k, t), F32)))
    return out


def _route_body(q_ref, sk_ref, a_ref, b_ref, ea_ref, eb_ref, tau_ref):
    half = PEER_DK // 2

    def head(h, _):
        r0 = pl.multiple_of(h * PEER_NKEYS, PEER_NKEYS)
        c0 = pl.multiple_of(h * PEER_DK, PEER_DK)
        s1 = _nt(sk_ref[h, 0], q_ref[:, pl.ds(c0, half)].astype(BF))
        s2 = _nt(sk_ref[h, 1], q_ref[:, pl.ds(c0 + half, half)].astype(BF))
        a_ref[0, pl.ds(r0, PEER_NKEYS), :] = s1
        b_ref[0, pl.ds(r0, PEER_NKEYS), :] = s2
        t1 = _top_desc(s1, PEER_TOPK)
        t2 = _top_desc(s2, PEER_TOPK)
        cand = jnp.concatenate([t1[r:r + 1] + t2 for r in range(PEER_TOPK)], axis=0)
        tv = _top_desc(cand, PEER_TOPK)
        zsum = jnp.sum(jnp.exp(tv - tv[0:1]), axis=0, keepdims=True)
        ea_ref[0, pl.ds(r0, PEER_NKEYS), :] = jnp.exp(s1 - t1[0:1]) / zsum
        eb_ref[0, pl.ds(r0, PEER_NKEYS), :] = jnp.exp(s2 - t2[0:1])
        tau_ref[0, pl.ds(h, 1), :] = tv[PEER_TOPK - 1:PEER_TOPK]
        return 0

    lax.fori_loop(0, PEER_HEADS, head, 0)


def _route(q2d, sk):
    rows = q2d.shape[0]
    tt = PEER_TOK
    nblk = rows // tt
    big = jax.ShapeDtypeStruct((nblk, PEER_HEADS * PEER_NKEYS, tt), F32)
    bspec = pl.BlockSpec((1, PEER_HEADS * PEER_NKEYS, tt), lambda i: (i, 0, 0))
    return pl.pallas_call(
        _route_body,
        grid=(nblk,),
        in_specs=[pl.BlockSpec((tt, PEER_HEADS * PEER_DK), lambda i: (i, 0)),
                  pl.BlockSpec(sk.shape, lambda i: (0, 0, 0, 0))],
        out_specs=[bspec, bspec, bspec, bspec, pl.BlockSpec((1, PEER_HEADS, tt), lambda i: (i, 0, 0))],
        out_shape=[big, big, big, big, jax.ShapeDtypeStruct((nblk, PEER_HEADS, tt), F32)],
        compiler_params=_params(("parallel",), 48),
        name="route",
    )(q2d, sk)


def _peer_body(h_ref, u_ref, v_ref, a_ref, b_ref, ea_ref, eb_ref, tau_ref, x_ref, gt_ref, g_ref, bb_ref,
               o_ref, acc_ref, pre_ref, act_ref, *, te, tt):
    j = pl.program_id(1)
    slab = 32

    @pl.when(j == 0)
    def _():
        acc_ref[...] = jnp.zeros(acc_ref.shape, F32)

    pre_ref[...] = _nt(u_ref[...], h_ref[...])

    def do_slab(sidx, _):
        r0 = pl.multiple_of(sidx * slab, slab)
        a = j * (te // PEER_NKEYS) + (sidx * slab) // PEER_NKEYS
        b0 = (sidx * slab) % PEER_NKEYS
        gm = jnp.zeros((slab, tt), F32)
        for h in range(PEER_HEADS):
            arow = a_ref[0, pl.ds(h * PEER_NKEYS + a, 1), :]
            earow = ea_ref[0, pl.ds(h * PEER_NKEYS + a, 1), :]
            bs = b_ref[0, pl.ds(pl.multiple_of(h * PEER_NKEYS + b0, slab), slab), :]
            ebs = eb_ref[0, pl.ds(pl.multiple_of(h * PEER_NKEYS + b0, slab), slab), :]
            tau = tau_ref[0, h:h + 1, :]
            gm = gm + jnp.where(arow + bs >= tau, earow * ebs, 0.0)
        pre = pre_ref[pl.ds(r0, slab), :]
        act_ref[pl.ds(r0, slab), :] = (jax.nn.gelu(pre) * gm).astype(BF)
        return 0

    lax.fori_loop(0, te // slab, do_slab, 0)
    acc_ref[...] += _tn(act_ref[...], v_ref[...])

    @pl.when(j == pl.num_programs(1) - 1)
    def _():
        y = DEEPNORM_ALPHA * x_ref[...] + gt_ref[0] * acc_ref[...]
        o_ref[...] = _layer_norm(y, g_ref[...], bb_ref[...])


def _peer(h2, u, v, a, b, ea, eb, tau, x1, gate, g, bb, rows_per_gate):
    rows, d = x1.shape
    tt, te = PEER_TOK, PEER_ETILE
    nblk = rows // tt
    one = pl.Buffered(1)
    rspec = pl.BlockSpec((1, PEER_HEADS * PEER_NKEYS, tt), lambda i, j: (i, 0, 0), pipeline_mode=one)
    if rows_per_gate == 1:
        gspec = pl.BlockSpec((1, tt, d), lambda i, j: (i, 0, 0), pipeline_mode=one)
    else:
        gspec = pl.BlockSpec((1, 1, d), lambda i, j: (i * tt // rows_per_gate, 0, 0))
    vec = pl.BlockSpec((1, d), lambda i, j: (0, 0))
    return pl.pallas_call(
        functools.partial(_peer_body, te=te, tt=tt),
        grid=(nblk, PEER_EXPERTS // te),
        in_specs=[pl.BlockSpec((tt, d), lambda i, j: (i, 0), pipeline_mode=one),
                  pl.BlockSpec((te, d), lambda i, j: (j, 0)),
                  pl.BlockSpec((te, d), lambda i, j: (j, 0)),
                  rspec, rspec, rspec, rspec,
                  pl.BlockSpec((1, PEER_HEADS, tt), lambda i, j: (i, 0, 0)),
                  pl.BlockSpec((tt, d), lambda i, j: (i, 0), pipeline_mode=one),
                  gspec, vec, vec],
        out_specs=pl.BlockSpec((tt, d), lambda i, j: (i, 0)),
        out_shape=jax.ShapeDtypeStruct((rows, d), F32),
        scratch_shapes=[pltpu.VMEM((tt, d), F32), pltpu.VMEM((te, tt), F32), pltpu.VMEM((te, tt), BF)],
        compiler_params=_params(("parallel", "arbitrary"), 56),
        name="peer",
    )(h2, u, v, a, b, ea, eb, tau, x1, gate, g, bb)


def _pad_rows(x, rows):
    return jnp.concatenate([x, jnp.zeros((rows - x.shape[0], x.shape[1]), x.dtype)], axis=0)


def _nsa_s_body(pt_ref, *refs, npg):
    del pt_ref
    kp = refs[:npg]
    vp = refs[npg:2 * npg]
    (z_ref, ga_ref, w1_ref, pos_ref, w2_ref, bc_ref, bs_ref, ovl_ref, oc_ref, os_ref, xc_ref) = refs[2 * npg:]
    past = npg * LANES
    nb = past // CMP_STRIDE - 1
    ns = -(-(past + 4) // SEL_BLOCK)
    rows = SAMPLE_ROWS
    ga = ga_ref[0]
    tpos = past + lax.broadcasted_iota(jnp.int32, (rows, 1), 0)
    rowi = lax.broadcasted_iota(jnp.int32, (rows, LANES), 0)
    lane = lax.broadcasted_iota(jnp.int32, (rows, LANES), 1)
    cpp = LANES // CMP_STRIDE
    rstride = 2 * NSA_KV
    for g in range(NSA_KV):
        kvc = []
        for kind, pages in enumerate((kp, vp)):
            for p in range(npg):
                for s in range(CMP_STRIDE):
                    xc_ref[p * cpp:(p + 1) * cpp, s * HEAD_DIM:(s + 1) * HEAD_DIM] = (
                        pages[p][0, pl.ds(rstride * s + g, cpp, stride=rstride * CMP_STRIDE), :])
            kvc.append(_compress_core(xc_ref[...].astype(BF), w1_ref, pos_ref, w2_ref, kind))
        qs = [z_ref[0, :, QA0 + (g * NSA_REP + r) * HEAD_DIM:QA0 + (g * NSA_REP + r + 1) * HEAD_DIM].astype(BF)
              for r in range(NSA_REP)]
        outs, sel = _cmp_sel_math(qs, kvc[0], kvc[1], bc_ref[g * NSA_REP:(g + 1) * NSA_REP], tpos, nb, ns,
                                  ovl_ref[...])
        for r in range(NSA_REP):
            hd = g * NSA_REP + r
            oc_ref[0, :, hd * HEAD_DIM:(hd + 1) * HEAD_DIM] = outs[r] * ga[:, hd * 3:hd * 3 + 1]
        qst = jnp.concatenate(qs, axis=0)
        ks0 = KV0 + 2 * NSA_KV * HEAD_DIM + g * HEAD_DIM
        vs0 = KV0 + 3 * NSA_KV * HEAD_DIM + g * HEAD_DIM
        knew = _pad_rows(z_ref[0, :, ks0:ks0 + HEAD_DIM], LANES).astype(BF)
        vnew = _pad_rows(z_ref[0, :, vs0:vs0 + HEAD_DIM], LANES).astype(BF)
        bpp = LANES // SEL_BLOCK
        s_tiles, m_tiles = [], []
        for p in range(npg + 1):
            if p < npg:
                kt = kp[p][0, pl.ds(NSA_KV + g, LANES, stride=rstride), :].astype(BF)
                mk = jnp.zeros((rows, LANES), F32)
                for bq in range(bpp):
                    in_blk = (lane >= bq * SEL_BLOCK) & (lane < (bq + 1) * SEL_BLOCK)
                    mk = jnp.where(in_blk, sel[:, p * bpp + bq:p * bpp + bq + 1], mk)
                mk = mk > 0.5
            else:
                kt = knew
                mk = (lane <= rowi) & (lane < rows) & (sel[:, ns - 1:ns] > 0.5)
            bias = bs_ref[g * NSA_REP:(g + 1) * NSA_REP, :, p * LANES:(p + 1) * LANES].reshape(NSA_REP * rows, LANES)
            mk = jnp.concatenate([mk] * NSA_REP, axis=0)
            s_tiles.append(jnp.where(mk, _nt(qst, kt) * SCALE + bias, NEG_INF))
            m_tiles.append(mk)
        mx = s_tiles[0].max(axis=-1, keepdims=True)
        for st in s_tiles[1:]:
            mx = jnp.maximum(mx, st.max(axis=-1, keepdims=True))
        den = jnp.zeros((NSA_REP * rows, 1), F32)
        acc = jnp.zeros((NSA_REP * rows, HEAD_DIM), F32)
        for p in range(npg + 1):
            pr = jnp.exp(s_tiles[p] - mx) * m_tiles[p].astype(F32)
            den = den + jnp.sum(pr, axis=-1, keepdims=True)
            if p < npg:
                vt = vp[p][0, pl.ds(NSA_KV + g, LANES, stride=rstride), :].astype(BF)
            else:
                vt = vnew
            acc = acc + _dot(pr.astype(BF), vt)
        o = acc / jnp.maximum(den, 1e-30)
        for r in range(NSA_REP):
            hd = g * NSA_REP + r
            os_ref[0, :, hd * HEAD_DIM:(hd + 1) * HEAD_DIM] = o[r * rows:(r + 1) * rows] * ga[:, hd * 3 + 1:hd * 3 + 2]


def _nsa_s(pt, ck, cv, zs, ga, w1, pos, w2, bias_c, bias_s, ovl):
    bsz, npg = pt.shape
    rows = SAMPLE_ROWS
    prow = ck.shape[1]
    page = lambda p: pl.BlockSpec((1, prow, HEAD_DIM), functools.partial(lambda b, t, p: (t[b, p], 0, 0), p=p))
    const = lambda shp: pl.BlockSpec(shp, lambda b, t: (0,) * len(shp))
    in_specs = ([page(p) for p in range(npg)] + [page(p) for p in range(npg)]
                + [pl.BlockSpec((1, rows, PROJ_COLS), lambda b, t: (b, 0, 0)),
                   pl.BlockSpec((1, rows, LANES), lambda b, t: (b, 0, 0)),
                   const(w1.shape), const(pos.shape), const(w2.shape), const(bias_c.shape), const(bias_s.shape),
                   const(ovl.shape)])
    ospec = pl.BlockSpec((1, rows, NSA_WIDTH), lambda b, t: (b, 0, 0))
    gs = pltpu.PrefetchScalarGridSpec(
        num_scalar_prefetch=1, grid=(bsz,), in_specs=in_specs, out_specs=[ospec, ospec],
        scratch_shapes=[pltpu.VMEM((npg * LANES // CMP_STRIDE, CMP_STRIDE * HEAD_DIM), F32)])
    sds = jax.ShapeDtypeStruct((bsz, rows, NSA_WIDTH), F32)
    return pl.pallas_call(
        functools.partial(_nsa_s_body, npg=npg),
        grid_spec=gs, out_shape=[sds, sds],
        compiler_params=_params(("parallel",), 56),
        name="nsa_s",
    )(pt, *([ck] * npg), *([cv] * npg), zs, ga, w1, pos, w2, bias_c, bias_s, ovl)


def _win_s_body(sk_ref, sv_ref, z_ref, ga_ref, bw_ref, o_ref, nk_ref, nv_ref, *, wbuf, nnew):
    rows = SAMPLE_ROWS
    ga = ga_ref[0]
    rowi = lax.broadcasted_iota(jnp.int32, (rows, LANES), 0)
    lane = lax.broadcasted_iota(jnp.int32, (rows, LANES), 1)
    rowk = lax.broadcasted_iota(jnp.int32, (rows, wbuf), 0)
    colk = lax.broadcasted_iota(jnp.int32, (rows, wbuf), 1)
    dist_old = wbuf + rowk - colk
    m_old = jnp.concatenate([(dist_old >= 0) & (dist_old <= WINDOW)] * NSA_REP, axis=0)
    m_new = jnp.concatenate([(lane <= rowi) & (lane < rows) & (rowi - lane <= WINDOW)] * NSA_REP, axis=0)
    for g in range(NSA_KV):
        qst = jnp.concatenate(
            [z_ref[0, :, QA0 + (g * NSA_REP + r) * HEAD_DIM:QA0 + (g * NSA_REP + r + 1) * HEAD_DIM]
             for r in range(NSA_REP)], axis=0).astype(BF)
        kw0 = KV0 + 4 * NSA_KV * HEAD_DIM + g * HEAD_DIM
        vw0 = KV0 + 5 * NSA_KV * HEAD_DIM + g * HEAD_DIM
        kold = sk_ref[0, pl.ds(g, wbuf, stride=NSA_KV), :].astype(BF)
        vold = sv_ref[0, pl.ds(g, wbuf, stride=NSA_KV), :].astype(BF)
        knew = _pad_rows(z_ref[0, :, kw0:kw0 + HEAD_DIM], LANES).astype(BF)
        vnew = _pad_rows(z_ref[0, :, vw0:vw0 + HEAD_DIM], LANES).astype(BF)
        b_old = bw_ref[g * NSA_REP:(g + 1) * NSA_REP, :, 0:wbuf].reshape(NSA_REP * rows, wbuf)
        b_new = bw_ref[g * NSA_REP:(g + 1) * NSA_REP, :, wbuf:wbuf + LANES].reshape(NSA_REP * rows, LANES)
        s_old = jnp.where(m_old, _nt(qst, kold) * SCALE + b_old, NEG_INF)
        s_new = jnp.where(m_new, _nt(qst, knew) * SCALE + b_new, NEG_INF)
        mx = jnp.maximum(s_old.max(axis=-1, keepdims=True), s_new.max(axis=-1, keepdims=True))
        p_old = jnp.exp(s_old - mx) * m_old.astype(F32)
        p_new = jnp.exp(s_new - mx) * m_new.astype(F32)
        den = jnp.sum(p_old, axis=-1, keepdims=True) + jnp.sum(p_new, axis=-1, keepdims=True)
        o = (_dot(p_old.astype(BF), vold) + _dot(p_new.astype(BF), vnew)) / jnp.maximum(den, 1e-30)
        for r in range(NSA_REP):
            hd = g * NSA_REP + r
            o_ref[0, :, hd * HEAD_DIM:(hd + 1) * HEAD_DIM] = o[r * rows:(r + 1) * rows] * ga[:, hd * 3 + 2:hd * 3 + 3]
    keep = (wbuf - nnew) * NSA_KV
    nk_ref[0, 0:keep, :] = sk_ref[0, nnew * NSA_KV:wbuf * NSA_KV, :]
    nv_ref[0, 0:keep, :] = sv_ref[0, nnew * NSA_KV:wbuf * NSA_KV, :]
    for (c0, dst) in ((KV0 + 4 * NSA_KV * HEAD_DIM, nk_ref), (KV0 + 5 * NSA_KV * HEAD_DIM, nv_ref)):
        new = jnp.concatenate(
            [z_ref[0, i:i + 1, c0 + g * HEAD_DIM:c0 + (g + 1) * HEAD_DIM] for i in range(nnew) for g in range(NSA_KV)],
            axis=0)
        dst[0, keep:wbuf * NSA_KV, :] = new


def _win_s(st_k, st_v, layer, zs, ga, bias_w, nnew):
    bsz = zs.shape[0]
    rows = SAMPLE_ROWS
    srows = st_k.shape[1]
    wbuf = srows // NSA_KV
    sspec = pl.BlockSpec((1, srows, HEAD_DIM), lambda b: (layer * bsz + b, 0, 0))
    nspec = pl.BlockSpec((1, srows, HEAD_DIM), lambda b: (b, 0, 0))
    nsds = jax.ShapeDtypeStruct((bsz, srows, HEAD_DIM), F32)
    return pl.pallas_call(
        functools.partial(_win_s_body, wbuf=wbuf, nnew=nnew),
        grid=(bsz,),
        in_specs=[sspec, sspec,
                  pl.BlockSpec((1, rows, PROJ_COLS), lambda b: (b, 0, 0)),
                  pl.BlockSpec((1, rows, LANES), lambda b: (b, 0, 0)),
                  pl.BlockSpec(bias_w.shape, lambda b: (0, 0, 0))],
        out_specs=[pl.BlockSpec((1, rows, NSA_WIDTH), lambda b: (b, 0, 0)), nspec, nspec],
        out_shape=[jax.ShapeDtypeStruct((bsz, rows, NSA_WIDTH), F32), nsds, nsds],
        compiler_params=_params(("parallel",), 32),
        name="win_s",
    )(st_k, st_v, zs, ga, bias_w)


def _fox_s_body(pt_ref, *refs, npg):
    del pt_ref
    kp = refs[:npg]
    vp = refs[npg:2 * npg]
    fp = refs[2 * npg:3 * npg]
    z_ref, lf_ref, o_ref = refs[3 * npg:]
    rows = SAMPLE_ROWS
    rowi = lax.broadcasted_iota(jnp.int32, (rows, LANES), 0)
    lane = lax.broadcasted_iota(jnp.int32, (rows, LANES), 1)
    r128 = lax.broadcasted_iota(jnp.int32, (LANES, LANES), 0)
    c128 = lax.broadcasted_iota(jnp.int32, (LANES, LANES), 1)
    upper = (r128 <= c128).astype(BF)
    ft = []
    carry = jnp.zeros((LANES, 1), F32)
    for p in range(npg):
        lp = fp[p][0]
        lpad = jnp.concatenate([lp, jnp.zeros((LANES, LANES - FOX_HEADS), F32)], axis=1)
        hi, mid, lo = _split3(lpad)
        cs = _tn(hi, upper) + _tn(mid, upper) + _tn(lo, upper) + carry
        carry = cs[:, LANES - 1:LANES]
        ft.append(cs[0:FOX_HEADS])
    lf_new = lf_ref[0]
    x = lf_new
    for sh in (1, 2, 4):
        x = x + jnp.where(rowi >= sh, pltpu.roll(x, sh, 0), 0.0)
    tot_t = jnp.broadcast_to(carry, (LANES, LANES)).T
    tot_row = pltpu.roll(tot_t[0:1, :], FB_LANE, 1)
    f_new = x + tot_row
    f_new_t = _pad_rows(f_new, LANES).T
    m_new = (lane <= rowi) & (lane < rows)
    for h in range(FOX_HEADS):
        q = z_ref[0, :, QB0 + h * HEAD_DIM:QB0 + (h + 1) * HEAD_DIM].astype(BF)
        k0 = QB0 + FOX_WIDTH + h * HEAD_DIM
        v0 = QB0 + 2 * FOX_WIDTH + h * HEAD_DIM
        knew = _pad_rows(z_ref[0, :, k0:k0 + HEAD_DIM], LANES).astype(BF)
        vnew = _pad_rows(z_ref[0, :, v0:v0 + HEAD_DIM], LANES).astype(BF)
        fq = f_new[:, FB_LANE + h:FB_LANE + h + 1]
        s_tiles = []
        for p in range(npg):
            kt = kp[p][0, pl.ds(h, LANES, stride=FOX_HEADS), :].astype(BF)
            s_tiles.append(_nt(q, kt) * SCALE + (fq - ft[p][h:h + 1, :]))
        s_n = _nt(q, knew) * SCALE + (fq - f_new_t[FB_LANE + h:FB_LANE + h + 1, :])
        s_tiles.append(jnp.where(m_new, s_n, NEG_INF))
        mx = s_tiles[0].max(axis=-1, keepdims=True)
        for st in s_tiles[1:]:
            mx = jnp.maximum(mx, st.max(axis=-1, keepdims=True))
        den = jnp.zeros((rows, 1), F32)
        acc = jnp.zeros((rows, HEAD_DIM), F32)
        for p in range(npg + 1):
            pr = jnp.exp(s_tiles[p] - mx)
            if p == npg:
                pr = pr * m_new.astype(F32)
                vt = vnew
            else:
                vt = vp[p][0, pl.ds(h, LANES, stride=FOX_HEADS), :].astype(BF)
            den = den + jnp.sum(pr, axis=-1, keepdims=True)
            acc = acc + _dot(pr.astype(BF), vt)
        o_ref[0, :, h * HEAD_DIM:(h + 1) * HEAD_DIM] = acc / jnp.maximum(den, 1e-30)


def _fox_s(pt, ck, cv, cf, zs, lf):
    bsz, npg = pt.shape
    rows = SAMPLE_ROWS
    prow = ck.shape[1]
    pmap = lambda p: functools.partial(lambda b, t, p: (t[b, p], 0, 0), p=p)
    in_specs = ([pl.BlockSpec((1, prow, HEAD_DIM), pmap(p)) for p in range(npg)]
                + [pl.BlockSpec((1, prow, HEAD_DIM), pmap(p)) for p in range(npg)]
                + [pl.BlockSpec((1, LANES, FOX_HEADS), pmap(p)) for p in range(npg)]
                + [pl.BlockSpec((1, rows, PROJ_COLS), lambda b, t: (b, 0, 0)),
                   pl.BlockSpec((1, rows, LANES), lambda b, t: (b, 0, 0))])
    gs = pltpu.PrefetchScalarGridSpec(
        num_scalar_prefetch=1, grid=(bsz,), in_specs=in_specs,
        out_specs=pl.BlockSpec((1, rows, FOX_WIDTH), lambda b, t: (b, 0, 0)))
    return pl.pallas_call(
        functools.partial(_fox_s_body, npg=npg),
        grid_spec=gs, out_shape=jax.ShapeDtypeStruct((bsz, rows, FOX_WIDTH), F32),
        compiler_params=_params(("parallel",), 56),
        name="fox_s",
    )(pt, *([ck] * npg), *([cv] * npg), *([cf] * npg), zs, lf)


def _rel_bucket(dist):
    n = jnp.maximum(dist, 0)
    exact = REL_BUCKETS // 2
    scaled = jnp.log(jnp.maximum(n, 1).astype(F32) / exact) / math.log(REL_MAX_DIST / exact)
    large = jnp.minimum(exact + (scaled * (REL_BUCKETS - exact)).astype(jnp.int32), REL_BUCKETS - 1)
    return jnp.where(n < exact, n, large)


def _bias_of(rel_table, dist):
    return jnp.moveaxis(rel_table[_rel_bucket(dist)], -1, 0)


def _overlap(nb, ns):
    i = np.arange(LANES)[:, None]
    j = np.arange(LANES)[None, :]
    rs = SEL_BLOCK // CMP_STRIDE
    rc = CMP_LEN // CMP_STRIDE
    m = np.zeros((LANES, LANES), np.float32)
    for a in range(rs):
        for b in range(rc):
            m = m + (i == j * rs + a + b - (rc - 1))
    m = m * (i < nb) * (j < ns)
    return jnp.asarray(m, BF)


def kernel(x_prompt, x_sample, cache_nsa_k, cache_nsa_v, cache_fox_k, cache_fox_v, cache_fox_logf,
           state_nsa_win_k, state_nsa_win_v, page_table, c_prompt, c_sample, w_ada, b_ada, w_in,
           b_forget, cmp_pos, cmp_w1, cmp_w2, rel_table, w_branch_a, w_branch_b, w_out, ln_g, ln_b,
           peer_wq, peer_subkeys, peer_u, peer_v):
    bp, seq, d = x_prompt.shape
    bs, dec = x_sample.shape[:2]
    depth, n_pool, page = cache_nsa_k.shape[:3]
    npg = page_table.shape[1]
    past = npg * page
    wbuf = state_nsa_win_k.shape[2]
    rows = SAMPLE_ROWS
    assert d == D_MODEL and page == LANES and depth == DEPTH and dec <= rows
    assert seq % FLASH_T == 0 or seq < FLASH_T

    ck_nsa = cache_nsa_k.reshape(depth * n_pool, page * 2 * NSA_KV, HEAD_DIM)
    cv_nsa = cache_nsa_v.reshape(depth * n_pool, page * 2 * NSA_KV, HEAD_DIM)
    ck_fox = cache_fox_k.reshape(depth * n_pool, page * FOX_HEADS, HEAD_DIM)
    cv_fox = cache_fox_v.reshape(depth * n_pool, page * FOX_HEADS, HEAD_DIM)
    cf_fox = cache_fox_logf.reshape(depth * n_pool, page, FOX_HEADS)
    st_k = state_nsa_win_k.reshape(depth * bs, wbuf * NSA_KV, HEAD_DIM)
    st_v = state_nsa_win_v.reshape(depth * bs, wbuf * NSA_KV, HEAD_DIM)

    tq = min(FLASH_T, seq)
    ii = jnp.arange(tq)
    bias_tile = jnp.stack([_bias_of(rel_table, dd * tq + ii[:, None] - ii[None, :]) for dd in range(3)])
    nb_p = seq // CMP_STRIDE - 1
    ns_p = -(-seq // SEL_BLOCK)
    lane = jnp.arange(LANES)
    bias_cmp_p = _bias_of(rel_table, jnp.arange(seq)[:, None] - (lane * CMP_STRIDE + CMP_LEN - 1)[None, :])
    nb_s = past // CMP_STRIDE - 1
    ns_s = -(-(past + dec) // SEL_BLOCK)
    tq_s = past + jnp.arange(rows)
    bias_cmp_s = _bias_of(rel_table, tq_s[:, None] - (lane * CMP_STRIDE + CMP_LEN - 1)[None, :])
    bias_sel_s = _bias_of(rel_table, tq_s[:, None] - jnp.arange(past + LANES)[None, :])
    bias_win_s = _bias_of(rel_table, wbuf + jnp.arange(rows)[:, None] - jnp.arange(wbuf + LANES)[None, :])
    ovl_p = _overlap(nb_p, ns_p)
    ovl_s = _overlap(nb_s, ns_s)

    c_all = jnp.concatenate([c_prompt, c_sample], axis=0)
    xp = x_prompt
    xs = x_sample
    outs_p = {k: [] for k in ("nsa_k", "nsa_v", "fox_k", "fox_v", "logf", "win_k", "win_v")}
    outs_s = {k: [] for k in ("nsa_k", "nsa_v", "fox_k", "fox_v", "logf", "win_k", "win_v")}
    gh = NSA_KV * HEAD_DIM

    for l in range(depth):
        wl = w_in[l]
        o_ga = NSA_WIDTH + 6 * gh
        o_qb = o_ga + 3 * NSA_HEADS
        o_fb = o_qb + 3 * FOX_WIDTH
        o_gm = o_fb + FOX_HEADS
        wcat = jnp.concatenate(
            [wl[:, :o_ga], wl[:, o_qb:o_fb], wl[:, o_gm:], wl[:, o_ga:o_qb], wl[:, o_fb:o_gm],
             jnp.zeros((d, PROJ_COLS - GF0 - 3 * NSA_HEADS - FOX_HEADS), F32)], axis=1).astype(BF)
        bfpad = jnp.zeros((1, LANES), F32).at[0, FB_LANE:FB_LANE + FOX_HEADS].set(b_forget[l])
        w1 = cmp_w1[l].astype(BF)
        pos = cmp_pos[l].reshape(2, 1, CMP_LEN * HEAD_DIM)
        w2 = cmp_w2[l].astype(BF)
        wa = w_branch_a[l].astype(BF)
        wb = w_branch_b[l].astype(BF)
        wo = w_out[l].astype(BF)
        wq = peer_wq[l].astype(BF)
        sk = peer_subkeys[l].astype(BF)
        pu = peer_u[l].astype(BF)
        pv = peer_v[l].astype(BF)
        g1, b1 = ln_g[l, 0][None], ln_b[l, 0][None]
        g2, b2 = ln_g[l, 1][None], ln_b[l, 1][None]
        pt_l = page_table + l * n_pool

        mod = _ada(c_all, w_ada[l], b_ada[l][None]).reshape(bp + bs, 6, d)
        mod_p = mod[:bp]
        mod_s = mod[bp:]

        z = _proj(xp, mod_p[:, 0:1], mod_p[:, 1:2], wcat, tm=min(1024, seq), tn=1024)
        ga, lf, fcum = _gates(z, bfpad)
        kvc = _compress_p(z, w1, pos, w2)
        o_cmp, sel = _cmp_p(z, kvc, bias_cmp_p, ga, ovl_p, nb_p, ns_p)
        o_sel = _flash("sel", z, (bias_tile, ga, sel))
        o_win = _flash("win", z, (bias_tile, ga))
        f_t = jnp.swapaxes(fcum[:, :, FB_LANE:FB_LANE + FOX_HEADS], 1, 2)
        o_b = _flash("fox", z, (fcum, f_t))
        nrow = bp * seq
        u = _merge1(o_cmp.reshape(nrow, -1), o_sel.reshape(nrow, -1), o_win.reshape(nrow, -1),
                    o_b.reshape(nrow, -1), z.reshape(nrow, -1), wa, wb)
        x1 = _merge2(u.reshape(bp, seq, d), wo, xp, mod_p[:, 2:3], g1, b1)
        q, h2 = _proj(x1, mod_p[:, 3:4], mod_p[:, 4:5], wq, tm=min(1024, seq), tn=1024, emit_h=True)
        ra, rb, rea, reb, tau = _route(q.reshape(nrow, -1), sk)
        xp = _peer(h2.reshape(nrow, d), pu, pv, ra, rb, rea, reb, tau, x1.reshape(nrow, d),
                   mod_p[:, 5:6], g2, b2, rows_per_gate=seq).reshape(bp, seq, d)
        kv = z[:, :, KV0:KV0 + 6 * gh].reshape(bp, seq, 6, NSA_KV, HEAD_DIM)
        outs_p["nsa_k"].append(kv[:, :, 0:4:2])
        outs_p["nsa_v"].append(kv[:, :, 1:4:2])
        outs_p["fox_k"].append(z[:, :, QB0 + FOX_WIDTH:QB0 + 2 * FOX_WIDTH].reshape(bp, seq, FOX_HEADS, HEAD_DIM))
        outs_p["fox_v"].append(z[:, :, QB0 + 2 * FOX_WIDTH:QB0 + 3 * FOX_WIDTH].reshape(bp, seq, FOX_HEADS, HEAD_DIM))
        outs_p["logf"].append(lf[:, :, FB_LANE:FB_LANE + FOX_HEADS])
        keep = min(WINDOW, seq)
        outs_p["win_k"].append(kv[:, seq - keep:, 4])
        outs_p["win_v"].append(kv[:, seq - keep:, 5])

        xs8 = jnp.pad(xs, ((0, 0), (0, rows - dec), (0, 0)))
        mrow = lambda k: jnp.repeat(mod_s[:, k], rows, axis=0)[None]
        zs = _proj(xs8.reshape(1, bs * rows, d), mrow(0), mrow(1), wcat, tm=min(512, bs * rows), tn=1024)
        ga_s, lf_s, _ = _gates(zs, bfpad)
        zs = zs.reshape(bs, rows, PROJ_COLS)
        ga_s = ga_s.reshape(bs, rows, LANES)
        lf_s = lf_s.reshape(bs, rows, LANES)
        oc_s, os_s = _nsa_s(pt_l, ck_nsa, cv_nsa, zs, ga_s, w1, pos, w2, bias_cmp_s, bias_sel_s, ovl_s)
        ow_s, nk_s, nv_s = _win_s(st_k, st_v, l, zs, ga_s, bias_win_s, dec)
        ob_s = _fox_s(pt_l, ck_fox, cv_fox, cf_fox, zs, lf_s)
        srow = bs * rows
        u_s = _merge1(oc_s.reshape(srow, -1), os_s.reshape(srow, -1), ow_s.reshape(srow, -1),
                      ob_s.reshape(srow, -1), zs.reshape(srow, -1), wa, wb)
        x1_s = _merge2(u_s.reshape(1, srow, d), wo, xs8.reshape(1, srow, d), mrow(2), g1, b1)
        x1_c = x1_s.reshape(bs, rows, d)[:, :dec].reshape(1, bs * dec, d)
        crow = lambda k: jnp.repeat(mod_s[:, k], dec, axis=0)[None]
        q_s, h2_s = _proj(x1_c, crow(3), crow(4), wq, tm=min(512, bs * dec), tn=1024, emit_h=True)
        ra, rb, rea, reb, tau = _route(q_s.reshape(bs * dec, -1), sk)
        xs = _peer(h2_s.reshape(bs * dec, d), pu, pv, ra, rb, rea, reb, tau, x1_c.reshape(bs * dec, d),
                   crow(5).reshape(-1, PEER_TOK, d), g2, b2, rows_per_gate=1).reshape(bs, dec, d)
        kv_s = zs[:, :dec, KV0:KV0 + 6 * gh].reshape(bs, dec, 6, NSA_KV, HEAD_DIM)
        outs_s["nsa_k"].append(kv_s[:, :, 0:4:2])
        outs_s["nsa_v"].append(kv_s[:, :, 1:4:2])
        outs_s["fox_k"].append(zs[:, :dec, QB0 + FOX_WIDTH:QB0 + 2 * FOX_WIDTH].reshape(bs, dec, FOX_HEADS, HEAD_DIM))
        outs_s["fox_v"].append(zs[:, :dec, QB0 + 2 * FOX_WIDTH:QB0 + 3 * FOX_WIDTH].reshape(bs, dec, FOX_HEADS, HEAD_DIM))
        outs_s["logf"].append(lf_s[:, :dec, FB_LANE:FB_LANE + FOX_HEADS])
        outs_s["win_k"].append(nk_s.reshape(bs, wbuf, NSA_KV, HEAD_DIM))
        outs_s["win_v"].append(nv_s.reshape(bs, wbuf, NSA_KV, HEAD_DIM))

    st = lambda dct, k: jnp.stack(dct[k])
    return (xp, xs,
            st(outs_p, "nsa_k"), st(outs_p, "nsa_v"), st(outs_p, "fox_k"), st(outs_p, "fox_v"),
            st(outs_p, "logf"), st(outs_p, "win_k"), st(outs_p, "win_v"),
            st(outs_s, "nsa_k"), st(outs_s, "nsa_v"), st(outs_s, "fox_k"), st(outs_s, "fox_v"),
            st(outs_s, "logf"), st(outs_s, "win_k"), st(outs_s, "win_v"))
```

```python
import functools
import math

import numpy as np
import jax
import jax.numpy as jnp
from jax import lax
from jax.experimental import pallas as pl
from jax.experimental.pallas import tpu as pltpu

F32 = jnp.float32
BF = jnp.bfloat16

D_MODEL = 2048
DEPTH = 2
HEAD_DIM = 128
NSA_HEADS = D_MODEL // (2 * HEAD_DIM)
NSA_KV = max(1, NSA_HEADS // 4)
NSA_REP = NSA_HEADS // NSA_KV
FOX_HEADS = D_MODEL // (2 * HEAD_DIM)
NSA_WIDTH = NSA_HEADS * HEAD_DIM
FOX_WIDTH = FOX_HEADS * HEAD_DIM
CMP_LEN = 32
CMP_STRIDE = 16
CMP_HID = 256
SEL_BLOCK = 64
SEL_TOPN = 16
WINDOW = 512
REL_BUCKETS = 32
REL_MAX_DIST = 128
PEER_HEADS = 8
PEER_NKEYS = 128
PEER_EXPERTS = PEER_NKEYS * PEER_NKEYS
PEER_DK = 256
PEER_TOPK = 16
DEEPNORM_ALPHA = (2 * DEPTH) ** 0.25
LN_EPS = 1e-5
NEG_INF = -1e9
FORCE_SCORE = 1e9
SCALE = HEAD_DIM ** -0.5

LANES = 128
SUB = 8
VMEM_MB = 1024 * 1024

QA0 = 0
KV0 = QA0 + NSA_WIDTH
QB0 = KV0 + 6 * NSA_KV * HEAD_DIM
GM0 = QB0 + 3 * FOX_WIDTH
GF0 = GM0 + 2 * D_MODEL
PROJ_COLS = 10240
FB_LANE = 3 * NSA_HEADS
SAMPLE_ROWS = 8
PEER_TOK = 512
PEER_ETILE = 512
FLASH_T = 256
FOX_T = 512


def _nt(a, b):
    return lax.dot_general(a, b, (((1,), (1,)), ((), ())), preferred_element_type=F32)


def _tn(a, b):
    return lax.dot_general(a, b, (((0,), (0,)), ((), ())), preferred_element_type=F32)


def _dot(a, b):
    return jnp.dot(a, b, preferred_element_type=F32)


def _split3(x):
    hi = x.astype(BF)
    r1 = x - hi.astype(F32)
    mid = r1.astype(BF)
    lo = (r1 - mid.astype(F32)).astype(BF)
    return hi, mid, lo


def _params(sem, mb):
    return pltpu.CompilerParams(dimension_semantics=sem, vmem_limit_bytes=mb * VMEM_MB)


def _lane_col(x, idx):
    lane = lax.broadcasted_iota(jnp.int32, x.shape, 1)
    return jnp.sum(jnp.where(lane == idx, x, 0.0), axis=-1, keepdims=True)


def _ada_body(c_ref, w_ref, b_ref, o_ref):
    c = c_ref[...]
    a = (c * jax.nn.sigmoid(c)).astype(BF)
    o_ref[...] = _dot(a, w_ref[...].astype(BF)) + b_ref[...]


def _ada(c, w, b):
    m, d = c.shape
    n = w.shape[1]
    tn = 1536
    return pl.pallas_call(
        _ada_body,
        grid=(n // tn,),
        in_specs=[pl.BlockSpec((m, d), lambda j: (0, 0)),
                  pl.BlockSpec((d, tn), lambda j: (0, j)),
                  pl.BlockSpec((1, tn), lambda j: (0, j))],
        out_specs=pl.BlockSpec((m, tn), lambda j: (0, j)),
        out_shape=jax.ShapeDtypeStruct((m, n), F32),
        compiler_params=_params(("parallel",), 48),
        name="ada",
    )(c, w, b)


def _proj_body(x_ref, sh_ref, sc_ref, w_ref, o_ref, *rest, emit_h):
    if emit_h:
        hb_ref, h_ref = rest
    else:
        (h_ref,) = rest

    @pl.when(pl.program_id(2) == 0)
    def _():
        h = (x_ref[0] * (1.0 + sc_ref[0]) + sh_ref[0]).astype(BF)
        h_ref[...] = h
        if emit_h:
            hb_ref[0] = h

    o_ref[0] = _dot(h_ref[...], w_ref[...])


def _proj(x, shift, scale, w, tm, tn, emit_h=False):
    n, t, d = x.shape
    c = w.shape[1]
    ts = shift.shape[1]
    tms = tm if ts == t else 1
    mod_map = (lambda b, i, j: (b, i, 0)) if ts == t else (lambda b, i, j: (b, 0, 0))
    out_shape = [jax.ShapeDtypeStruct((n, t, c), F32)]
    out_specs = [pl.BlockSpec((1, tm, tn), lambda b, i, j: (b, i, j))]
    if emit_h:
        out_shape.append(jax.ShapeDtypeStruct((n, t, d), BF))
        out_specs.append(pl.BlockSpec((1, tm, d), lambda b, i, j: (b, i, 0)))
    res = pl.pallas_call(
        functools.partial(_proj_body, emit_h=emit_h),
        grid=(n, t // tm, c // tn),
        in_specs=[pl.BlockSpec((1, tm, d), lambda b, i, j: (b, i, 0)),
                  pl.BlockSpec((1, tms, d), mod_map),
                  pl.BlockSpec((1, tms, d), mod_map),
                  pl.BlockSpec((d, tn), lambda b, i, j: (0, j))],
        out_specs=out_specs,
        out_shape=out_shape,
        scratch_shapes=[pltpu.VMEM((tm, d), BF)],
        compiler_params=_params(("parallel", "parallel", "arbitrary"), 56),
        name="proj",
    )(x, shift, scale, w)
    return res if emit_h else res[0]


def _cumsum_rows(x, tri):
    hi, mid, lo = _split3(x)
    return _dot(tri, hi) + _dot(tri, mid) + _dot(tri, lo)


def _gates_body(z_ref, bf_ref, ga_ref, lf_ref, f_ref, *, t, blk):
    z = z_ref[0]
    ga_ref[0] = jax.nn.sigmoid(z)
    lf = jax.nn.log_sigmoid(z + bf_ref[...])
    lf_ref[0] = lf
    r = lax.broadcasted_iota(jnp.int32, (blk, blk), 0)
    c = lax.broadcasted_iota(jnp.int32, (blk, blk), 1)
    tri = (r >= c).astype(BF)
    carry = jnp.zeros((1, LANES), F32)
    for b in range(t // blk):
        fb = _cumsum_rows(lf[b * blk:(b + 1) * blk], tri) + carry
        f_ref[0, b * blk:(b + 1) * blk, :] = fb
        carry = fb[blk - 1:blk, :]


def _gates(z, bfpad):
    n, t, _ = z.shape
    blk = min(256, t)
    sds = jax.ShapeDtypeStruct((n, t, LANES), F32)
    spec = pl.BlockSpec((1, t, LANES), lambda b: (b, 0, 0))
    return pl.pallas_call(
        functools.partial(_gates_body, t=t, blk=blk),
        grid=(n,),
        in_specs=[pl.BlockSpec((1, t, LANES), lambda b: (b, 0, GF0 // LANES)),
                  pl.BlockSpec((1, LANES), lambda b: (0, 0))],
        out_specs=[spec, spec, spec],
        out_shape=[sds, sds, sds],
        compiler_params=_params(("parallel",), 32),
        name="gates",
    )(z, bfpad)


def _compress_core(xc, w1_ref, pos_ref, w2_ref, kind):
    m = xc.shape[0]
    half = CMP_STRIDE * HEAD_DIM
    first = _dot(xc, w1_ref[kind, 0:half, :])
    second = _dot(xc, w1_ref[kind, half:2 * half, :])
    posb = jnp.broadcast_to(pos_ref[kind], (SUB, 2 * half)).astype(BF)
    pos_term = _dot(posb, w1_ref[kind])[0:1]
    hid = jax.nn.gelu(first + pltpu.roll(second, m - 1, 0) + pos_term)
    return _dot(hid.astype(BF), w2_ref[kind])


def _compress_p_body(x_ref, w1_ref, pos_ref, w2_ref, o_ref, xc_ref, *, m):
    kind = pl.program_id(1) // NSA_KV
    for s in range(CMP_STRIDE):
        xc_ref[:, s * HEAD_DIM:(s + 1) * HEAD_DIM] = x_ref[0, pl.ds(s, m, stride=CMP_STRIDE), :].astype(BF)
    o_ref[0, 0] = _compress_core(xc_ref[...], w1_ref, pos_ref, w2_ref, kind)


def _compress_p(z, w1, pos, w2):
    n, t, _ = z.shape
    m = t // CMP_STRIDE
    return pl.pallas_call(
        functools.partial(_compress_p_body, m=m),
        grid=(n, 2 * NSA_KV),
        in_specs=[pl.BlockSpec((1, t, HEAD_DIM), lambda b, a: (b, 0, KV0 // HEAD_DIM + a)),
                  pl.BlockSpec(w1.shape, lambda b, a: (0, 0, 0)),
                  pl.BlockSpec(pos.shape, lambda b, a: (0, 0, 0)),
                  pl.BlockSpec(w2.shape, lambda b, a: (0, 0, 0))],
        out_specs=pl.BlockSpec((1, 1, m, HEAD_DIM), lambda b, a: (b, a, 0, 0)),
        out_shape=jax.ShapeDtypeStruct((n, 2 * NSA_KV, m, HEAD_DIM), F32),
        scratch_shapes=[pltpu.VMEM((m, CMP_STRIDE * HEAD_DIM), BF)],
        compiler_params=_params(("parallel", "parallel"), 32),
        name="compress_p",
    )(z, w1, pos, w2)


def _masked_softmax(s, mask):
    s = jnp.where(mask, s, NEG_INF)
    p = jnp.exp(s - jnp.max(s, axis=-1, keepdims=True)) * mask.astype(F32)
    return p / jnp.maximum(jnp.sum(p, axis=-1, keepdims=True), 1e-30)


def _cmp_sel_math(qs, kc, vc, bias, tpos, nb, ns, ovl):
    tq = qs[0].shape[0]
    lane = lax.broadcasted_iota(jnp.int32, (tq, LANES), 1)
    dist = tpos - (lane * CMP_STRIDE + (CMP_LEN - 1))
    mask = (dist >= 0) & (lane < nb)
    kcb = kc.astype(BF)
    vcb = vc.astype(BF)
    outs = []
    psum = jnp.zeros((tq, LANES), F32)
    for r in range(NSA_REP):
        p = _masked_softmax(_nt(qs[r], kcb) * SCALE + bias[r], mask)
        outs.append(_dot(p.astype(BF), vcb))
        psum = psum + p
    hi, mid, lo = _split3(psum)
    imp = _dot(hi, ovl) + _dot(mid, ovl) + _dot(lo, ovl)
    cur = tpos // SEL_BLOCK
    valid = lane * SEL_BLOCK <= tpos
    forced = (lane == 0) | (lane == cur) | (lane == cur - 1)
    score = jnp.where(forced, FORCE_SCORE, jnp.where(valid, imp, NEG_INF))
    score = jnp.where(lane < ns, score, -3.0e38)
    rank = jnp.zeros((tq, LANES), F32)
    for j in range(ns):
        cj = score[:, j:j + 1]
        ahead = (cj > score) | ((cj == score) & (lane > j))
        rank = rank + ahead.astype(F32)
    sel = (rank < float(min(SEL_TOPN, ns))) & (lane < ns)
    return outs, sel.astype(F32)


def _cmp_p_body(q_ref, kc_ref, vc_ref, bias_ref, ga_ref, ovl_ref, o_ref, sel_ref, *, tq, nb, ns):
    g = pl.program_id(1)
    qi = pl.program_id(2)
    q = q_ref[0]
    qs = [q[:, r * HEAD_DIM:(r + 1) * HEAD_DIM].astype(BF) for r in range(NSA_REP)]
    tpos = qi * tq + lax.broadcasted_iota(jnp.int32, (tq, 1), 0)
    outs, sel = _cmp_sel_math(qs, kc_ref[0, 0], vc_ref[0, 0], bias_ref[...], tpos, nb, ns, ovl_ref[...])
    ga = ga_ref[0]
    for r in range(NSA_REP):
        gate = _lane_col(ga, (g * NSA_REP + r) * 3 + 0)
        o_ref[0, :, r * HEAD_DIM:(r + 1) * HEAD_DIM] = outs[r] * gate
    sel_ref[0, 0] = sel.T


def _cmp_p(z, kvc, bias_cmp, ga, ovl, nb, ns):
    n, t, _ = z.shape
    tq = min(FLASH_T, t)
    gw = NSA_REP * HEAD_DIM
    m = kvc.shape[2]
    return pl.pallas_call(
        functools.partial(_cmp_p_body, tq=tq, nb=nb, ns=ns),
        grid=(n, NSA_KV, t // tq),
        in_specs=[pl.BlockSpec((1, tq, gw), lambda b, g, i: (b, i, g)),
                  pl.BlockSpec((1, 1, m, HEAD_DIM), lambda b, g, i: (b, g, 0, 0)),
                  pl.BlockSpec((1, 1, m, HEAD_DIM), lambda b, g, i: (b, NSA_KV + g, 0, 0)),
                  pl.BlockSpec((NSA_REP, tq, LANES), lambda b, g, i: (g, i, 0)),
                  pl.BlockSpec((1, tq, LANES), lambda b, g, i: (b, i, 0)),
                  pl.BlockSpec((LANES, LANES), lambda b, g, i: (0, 0))],
        out_specs=[pl.BlockSpec((1, tq, gw), lambda b, g, i: (b, i, g)),
                   pl.BlockSpec((1, 1, LANES, tq), lambda b, g, i: (b, g, 0, i))],
        out_shape=[jax.ShapeDtypeStruct((n, t, NSA_WIDTH), F32),
                   jax.ShapeDtypeStruct((n, NSA_KV, LANES, t), F32)],
        compiler_params=_params(("parallel", "parallel", "parallel"), 32),
        name="cmp_p",
    )(z, kvc, kvc, bias_cmp, ga, ovl)


def _flash_body(tab_ref, *refs, mode, tq):
    if mode == "sel":
        q_ref, k_ref, v_ref, bias_ref, ga_ref, sel_ref, o_ref, m_ref, l_ref, acc_ref = refs
    elif mode == "win":
        q_ref, k_ref, v_ref, bias_ref, ga_ref, o_ref, m_ref, l_ref, acc_ref = refs
    else:
        q_ref, k_ref, v_ref, fcol_ref, frow_ref, o_ref, m_ref, l_ref, acc_ref = refs
    rep = 1 if mode == "fox" else NSA_REP
    hg = pl.program_id(1)
    pair = pl.program_id(2)
    qi = tab_ref[0, pair]
    tile = tab_ref[1, pair]

    @pl.when(tab_ref[2, pair] == 1)
    def _():
        m_ref[...] = jnp.full(m_ref.shape, NEG_INF, F32)
        l_ref[...] = jnp.zeros(l_ref.shape, F32)
        acc_ref[...] = jnp.zeros(acc_ref.shape, F32)

    q = q_ref[0]
    if rep == 1:
        qs = q.astype(BF)
    else:
        qs = jnp.concatenate([q[:, r * HEAD_DIM:(r + 1) * HEAD_DIM] for r in range(rep)], axis=0).astype(BF)
    s = _nt(k_ref[0].astype(BF), qs) * SCALE
    ki = lax.broadcasted_iota(jnp.int32, (tq, tq), 0)
    qj = lax.broadcasted_iota(jnp.int32, (tq, tq), 1)
    dist = (qi - tile) * tq + qj - ki
    if mode == "sel":
        key_blk = (tile * tq + lax.broadcasted_iota(jnp.int32, (tq, LANES), 0)) // SEL_BLOCK
        expand = (key_blk == lax.broadcasted_iota(jnp.int32, (tq, LANES), 1)).astype(BF)
        hit = _dot(expand, sel_ref[0, 0].astype(BF))
        mask = (dist >= 0) & (hit > 0.5)
    elif mode == "win":
        mask = (dist >= 0) & (dist <= WINDOW)
    else:
        mask = dist >= 0
    if mode == "fox":
        fk = _lane_col(fcol_ref[0], FB_LANE + hg)
        s = s + (frow_ref[0, pl.ds(hg, 1), :] - fk)
    else:
        s = s + bias_ref[0, 0]
        mask = jnp.concatenate([mask] * rep, axis=1)
    s = jnp.where(mask, s, NEG_INF)
    m_old = m_ref[...]
    m_new = jnp.maximum(m_old, jnp.max(s, axis=0, keepdims=True))
    p = jnp.exp(s - m_new) * mask.astype(F32)
    alpha = jnp.exp(m_old - m_new)
    l_ref[...] = l_ref[...] * alpha + jnp.sum(p, axis=0, keepdims=True)
    acc_ref[...] = acc_ref[...] * alpha + _tn(v_ref[0].astype(BF), p.astype(BF))
    m_ref[...] = m_new

    @pl.when(tab_ref[3, pair] == 1)
    def _():
        o = acc_ref[...] / jnp.maximum(l_ref[...], 1e-30)
        if mode == "fox":
            o_ref[0] = o.T
        else:
            ga = ga_ref[0]
            branch = 1 if mode == "sel" else 2
            for r in range(rep):
                gate = _lane_col(ga, (hg * NSA_REP + r) * 3 + branch)
                o_ref[0, :, r * HEAD_DIM:(r + 1) * HEAD_DIM] = o[:, r * tq:(r + 1) * tq].T * gate


def _flash_tile(mode, t):
    return min(FOX_T if mode == "fox" else FLASH_T, t)


def _flash(mode, z, aux):
    n, t, _ = z.shape
    tq = _flash_tile(mode, t)
    nq = t // tq
    back = WINDOW // tq if mode == "win" else nq
    pairs = [(i, k) for i in range(nq) for k in range(max(i - back, 0), i + 1)]
    tab = jnp.asarray(np.array(
        [[i for i, _ in pairs], [k for _, k in pairs],
         [int(k == max(i - back, 0)) for i, k in pairs], [int(k == i) for i, k in pairs]], np.int32))
    if mode == "fox":
        heads, rep = FOX_HEADS, 1
        qc, kc, vc = QB0 // HEAD_DIM, (QB0 + FOX_WIDTH) // HEAD_DIM, (QB0 + 2 * FOX_WIDTH) // HEAD_DIM
    else:
        heads, rep = NSA_KV, NSA_REP
        off = 2 if mode == "sel" else 4
        qc = QA0 // (rep * HEAD_DIM)
        kc = KV0 // HEAD_DIM + off * NSA_KV
        vc = KV0 // HEAD_DIM + (off + 1) * NSA_KV
    w = rep * HEAD_DIM
    in_specs = [pl.BlockSpec((1, tq, w), lambda b, h, p, tb: (b, tb[0, p], qc + h)),
                pl.BlockSpec((1, tq, HEAD_DIM), lambda b, h, p, tb: (b, tb[1, p], kc + h)),
                pl.BlockSpec((1, tq, HEAD_DIM), lambda b, h, p, tb: (b, tb[1, p], vc + h))]
    if mode == "fox":
        f, ft = aux
        args = (z, z, z, f, ft)
        in_specs += [pl.BlockSpec((1, tq, LANES), lambda b, h, p, tb: (b, tb[1, p], 0)),
                     pl.BlockSpec((1, FOX_HEADS, tq), lambda b, h, p, tb: (b, 0, tb[0, p]))]
    else:
        bias, ga = aux[0], aux[1]
        args = (z, z, z, bias, ga)
        in_specs += [pl.BlockSpec((1, 1, tq, rep * tq),
                                  lambda b, h, p, tb: (jnp.minimum(tb[0, p] - tb[1, p], 2), h, 0, 0)),
                     pl.BlockSpec((1, tq, LANES), lambda b, h, p, tb: (b, tb[0, p], 0))]
        if mode == "sel":
            args += (aux[2],)
            in_specs += [pl.BlockSpec((1, 1, LANES, tq), lambda b, h, p, tb: (b, h, 0, tb[0, p]))]
    gs = pltpu.PrefetchScalarGridSpec(
        num_scalar_prefetch=1, grid=(n, heads, len(pairs)), in_specs=in_specs,
        out_specs=pl.BlockSpec((1, tq, w), lambda b, h, p, tb: (b, tb[0, p], h)),
        scratch_shapes=[pltpu.VMEM((1, rep * tq), F32), pltpu.VMEM((1, rep * tq), F32),
                        pltpu.VMEM((HEAD_DIM, rep * tq), F32)])
    return pl.pallas_call(
        functools.partial(_flash_body, mode=mode, tq=tq),
        grid_spec=gs,
        out_shape=jax.ShapeDtypeStruct((n, t, heads * w), F32),
        compiler_params=_params(("parallel", "parallel", "arbitrary"), 48),
        name="flash_" + mode,
    )(tab, *args)


def _merge1_body(oc_ref, os_ref, ow_ref, ob_ref, ga_ref, gb_ref, wa_ref, wb_ref, u_ref, oa_s, ob_s):
    @pl.when(pl.program_id(1) == 0)
    def _():
        oa_s[...] = (oc_ref[...] + os_ref[...] + ow_ref[...]).astype(BF)
        ob_s[...] = ob_ref[...].astype(BF)

    ya = _dot(oa_s[...], wa_ref[...])
    yb = _dot(ob_s[...], wb_ref[...])
    u_ref[...] = (jax.nn.sigmoid(ga_ref[...]) * ya + jax.nn.sigmoid(gb_ref[...]) * yb).astype(BF)


def _merge1(oc, osel, ow, ob, z2d, wa, wb):
    rows = oc.shape[0]
    tm, tn = 512, 512
    ospec = pl.BlockSpec((tm, NSA_WIDTH), lambda i, j: (i, 0))
    return pl.pallas_call(
        _merge1_body,
        grid=(rows // tm, D_MODEL // tn),
        in_specs=[ospec, ospec, ospec, ospec,
                  pl.BlockSpec((tm, tn), lambda i, j: (i, GM0 // tn + j)),
                  pl.BlockSpec((tm, tn), lambda i, j: (i, (GM0 + D_MODEL) // tn + j)),
                  pl.BlockSpec((NSA_WIDTH, tn), lambda i, j: (0, j)),
                  pl.BlockSpec((FOX_WIDTH, tn), lambda i, j: (0, j))],
        out_specs=pl.BlockSpec((tm, tn), lambda i, j: (i, j)),
        out_shape=jax.ShapeDtypeStruct((rows, D_MODEL), BF),
        scratch_shapes=[pltpu.VMEM((tm, NSA_WIDTH), BF), pltpu.VMEM((tm, FOX_WIDTH), BF)],
        compiler_params=_params(("parallel", "arbitrary"), 48),
        name="merge1",
    )(oc, osel, ow, ob, z2d, z2d, wa, wb)


def _layer_norm(y, g, b):
    mu = jnp.mean(y, axis=-1, keepdims=True)
    yc = y - mu
    var = jnp.mean(yc * yc, axis=-1, keepdims=True)
    return yc * lax.rsqrt(var + LN_EPS) * g + b


def _merge2_body(u_ref, wo_ref, x_ref, gt_ref, g_ref, b_ref, o_ref):
    mix = _dot(u_ref[0], wo_ref[...])
    y = DEEPNORM_ALPHA * x_ref[0] + gt_ref[0] * mix
    o_ref[0] = _layer_norm(y, g_ref[...], b_ref[...])


def _merge2(u, wo, x, gate, g, b):
    n, t, d = x.shape
    tm = min(512, t)
    ts = gate.shape[1]
    gspec = (pl.BlockSpec((1, tm, d), lambda bb, i: (bb, i, 0)) if ts == t
             else pl.BlockSpec((1, 1, d), lambda bb, i: (bb, 0, 0)))
    vec = pl.BlockSpec((1, d), lambda bb, i: (0, 0))
    return pl.pallas_call(
        _merge2_body,
        grid=(n, t // tm),
        in_specs=[pl.BlockSpec((1, tm, d), lambda bb, i: (bb, i, 0)),
                  pl.BlockSpec((d, d), lambda bb, i: (0, 0)),
                  pl.BlockSpec((1, tm, d), lambda bb, i: (bb, i, 0)),
                  gspec, vec, vec],
        out_specs=pl.BlockSpec((1, tm, d), lambda bb, i: (bb, i, 0)),
        out_shape=jax.ShapeDtypeStruct((n, t, d), F32),
        compiler_params=_params(("parallel", "parallel"), 56),
        name="merge2",
    )(u, wo, x, gate, g, b)


def _merge_desc(lst):
    n = len(lst)
    j = n // 2
    while j >= 1:
        for i in range(n):
            l = i ^ j
            if l > i:
                lst[i], lst[l] = jnp.maximum(lst[i], lst[l]), jnp.minimum(lst[i], lst[l])
        j //= 2
    return lst


def _sort_desc(lst):
    n = len(lst)
    k = 2
    while k <= n:
        j = k // 2
        while j >= 1:
            for i in range(n):
                l = i ^ j
                if l > i:
                    hi, lo = jnp.maximum(lst[i], lst[l]), jnp.minimum(lst[i], lst[l])
                    lst[i], lst[l] = (hi, lo) if (i & k) == 0 else (lo, hi)
            j //= 2
        k *= 2
    return lst


def _top_across_sublanes(lst):
    n = len(lst)
    for sh in (SUB // 2, SUB // 4, SUB // 8):
        part = [pltpu.roll(x, sh, 0) for x in lst]
        lst = _merge_desc([jnp.maximum(lst[i], part[n - 1 - i]) for i in range(n)])
    return lst


def _route_body(q_ref, sk_ref, a_ref, b_ref, ea_ref, eb_ref, tau_ref, *, tt):
    half = PEER_DK // 2
    k = PEER_TOPK
    groups = PEER_NKEYS // SUB

    def head(h, tau_all):
        r0 = pl.multiple_of(h * PEER_NKEYS, PEER_NKEYS)
        c0 = pl.multiple_of(h * PEER_DK, PEER_DK)
        a_ref[0, pl.ds(r0, PEER_NKEYS), :] = _nt(sk_ref[h, 0], q_ref[:, pl.ds(c0, half)].astype(BF))
        b_ref[0, pl.ds(r0, PEER_NKEYS), :] = _nt(sk_ref[h, 1], q_ref[:, pl.ds(c0 + half, half)].astype(BF))
        sub = lax.broadcasted_iota(jnp.int32, (SUB, LANES), 0)
        taus = []
        for c in range(tt // LANES):
            lanes = slice(c * LANES, (c + 1) * LANES)
            tops = []
            for ref in (a_ref, b_ref):
                rows = [ref[0, pl.ds(r0 + SUB * r, SUB), lanes] for r in range(groups)]
                tops.append(_top_across_sublanes(_sort_desc(rows)))
            t1, t2 = tops
            best = None
            for kk in range(k // SUB):
                spread = t2[SUB * kk]
                for s in range(1, SUB):
                    spread = jnp.where(sub == s, t2[SUB * kk + s], spread)
                cand = [t1[r] + spread for r in range(k)]
                best = cand if best is None else _merge_desc([jnp.maximum(best[i], cand[k - 1 - i]) for i in range(k)])
            tv = _top_across_sublanes(best)
            zsum = jnp.exp(tv[0] - tv[0])
            for i in range(1, k):
                zsum = zsum + jnp.exp(tv[i] - tv[0])
            s1 = a_ref[0, pl.ds(r0, PEER_NKEYS), lanes]
            s2 = b_ref[0, pl.ds(r0, PEER_NKEYS), lanes]
            ea_ref[0, pl.ds(r0, PEER_NKEYS), lanes] = jnp.exp(s1 - t1[0][0:1]) / zsum[0:1]
            eb_ref[0, pl.ds(r0, PEER_NKEYS), lanes] = jnp.exp(s2 - t2[0][0:1])
            taus.append(tv[k - 1])
        head_row = lax.broadcasted_iota(jnp.int32, (PEER_HEADS, tt), 0) == h
        return jnp.where(head_row, jnp.concatenate(taus, axis=1), tau_all)

    tau_ref[0] = lax.fori_loop(0, PEER_HEADS, head, jnp.zeros((PEER_HEADS, tt), F32))


def _route(q2d, sk):
    rows = q2d.shape[0]
    tt = PEER_TOK
    nblk = rows // tt
    big = jax.ShapeDtypeStruct((nblk, PEER_HEADS * PEER_NKEYS, tt), F32)
    bspec = pl.BlockSpec((1, PEER_HEADS * PEER_NKEYS, tt), lambda i: (i, 0, 0))
    return pl.pallas_call(
        functools.partial(_route_body, tt=tt),
        grid=(nblk,),
        in_specs=[pl.BlockSpec((tt, PEER_HEADS * PEER_DK), lambda i: (i, 0)),
                  pl.BlockSpec(sk.shape, lambda i: (0, 0, 0, 0))],
        out_specs=[bspec, bspec, bspec, bspec, pl.BlockSpec((1, PEER_HEADS, tt), lambda i: (i, 0, 0))],
        out_shape=[big, big, big, big, jax.ShapeDtypeStruct((nblk, PEER_HEADS, tt), F32)],
        compiler_params=_params(("parallel",), 48),
        name="route",
    )(q2d, sk)


def _peer_body(h_ref, u_ref, v_ref, a_ref, b_ref, ea_ref, eb_ref, tau_ref, x_ref, gt_ref, g_ref, bb_ref,
               o_ref, acc_ref, pre_ref, act_ref, *, te, tt):
    j = pl.program_id(1)
    nj = pl.num_programs(1) - 1
    slab = 32
    cur = j % 2
    tile = jnp.minimum(j, nj - 1)

    @pl.when(j == 0)
    def _():
        acc_ref[...] = jnp.zeros(acc_ref.shape, F32)
        act_ref[1] = jnp.zeros(act_ref.shape[1:], BF)

    acc_ref[...] += _tn(act_ref[1 - cur], v_ref[...])
    pre_ref[...] = _nt(u_ref[...], h_ref[...])
    for sidx in range(te // slab):
        a = tile * (te // PEER_NKEYS) + (sidx * slab) // PEER_NKEYS
        b0 = (sidx * slab) % PEER_NKEYS
        gm = None
        for h in range(PEER_HEADS):
            arow = a_ref[0, pl.ds(h * PEER_NKEYS + a, 1), :]
            earow = ea_ref[0, pl.ds(h * PEER_NKEYS + a, 1), :]
            bs = b_ref[0, h * PEER_NKEYS + b0:h * PEER_NKEYS + b0 + slab, :]
            ebs = eb_ref[0, h * PEER_NKEYS + b0:h * PEER_NKEYS + b0 + slab, :]
            term = jnp.where(arow + bs >= tau_ref[0, h:h + 1, :], earow * ebs, 0.0)
            gm = term if gm is None else gm + term
        pre = pre_ref[sidx * slab:(sidx + 1) * slab, :]
        act_ref[cur, sidx * slab:(sidx + 1) * slab, :] = (jax.nn.gelu(pre) * gm).astype(BF)

    @pl.when(j == nj)
    def _():
        y = DEEPNORM_ALPHA * x_ref[...] + gt_ref[0] * acc_ref[...]
        o_ref[...] = _layer_norm(y, g_ref[...], bb_ref[...])


def _peer(h2, u, v, a, b, ea, eb, tau, x1, gate, g, bb, rows_per_gate):
    rows, d = x1.shape
    tt, te = PEER_TOK, PEER_ETILE
    nblk = rows // tt
    nj = PEER_EXPERTS // te
    one = pl.Buffered(1)
    rspec = pl.BlockSpec((1, PEER_HEADS * PEER_NKEYS, tt), lambda i, j: (i, 0, 0), pipeline_mode=one)
    if rows_per_gate == 1:
        gspec = pl.BlockSpec((1, tt, d), lambda i, j: (i, 0, 0), pipeline_mode=one)
    else:
        gspec = pl.BlockSpec((1, 1, d), lambda i, j: (i * tt // rows_per_gate, 0, 0))
    vec = pl.BlockSpec((1, d), lambda i, j: (0, 0))
    return pl.pallas_call(
        functools.partial(_peer_body, te=te, tt=tt),
        grid=(nblk, nj + 1),
        in_specs=[pl.BlockSpec((tt, d), lambda i, j: (i, 0), pipeline_mode=one),
                  pl.BlockSpec((te, d), lambda i, j: (jnp.minimum(j, nj - 1), 0)),
                  pl.BlockSpec((te, d), lambda i, j: (jnp.maximum(j - 1, 0), 0)),
                  rspec, rspec, rspec, rspec,
                  pl.BlockSpec((1, PEER_HEADS, tt), lambda i, j: (i, 0, 0)),
                  pl.BlockSpec((tt, d), lambda i, j: (i, 0), pipeline_mode=one),
                  gspec, vec, vec],
        out_specs=pl.BlockSpec((tt, d), lambda i, j: (i, 0)),
        out_shape=jax.ShapeDtypeStruct((rows, d), F32),
        scratch_shapes=[pltpu.VMEM((tt, d), F32), pltpu.VMEM((te, tt), F32), pltpu.VMEM((2, te, tt), BF)],
        compiler_params=_params(("parallel", "arbitrary"), 56),
        name="peer",
    )(h2, u, v, a, b, ea, eb, tau, x1, gate, g, bb)


def _pad_rows(x, rows):
    return jnp.concatenate([x, jnp.zeros((rows - x.shape[0], x.shape[1]), x.dtype)], axis=0)


def _nsa_s_body(pt_ref, *refs, npg):
    del pt_ref
    kp = refs[:npg]
    vp = refs[npg:2 * npg]
    (z_ref, ga_ref, w1_ref, pos_ref, w2_ref, bc_ref, bs_ref, ovl_ref, oc_ref, os_ref, xc_ref) = refs[2 * npg:]
    past = npg * LANES
    nb = past // CMP_STRIDE - 1
    ns = -(-(past + 4) // SEL_BLOCK)
    rows = SAMPLE_ROWS
    ga = ga_ref[0]
    tpos = past + lax.broadcasted_iota(jnp.int32, (rows, 1), 0)
    rowi = lax.broadcasted_iota(jnp.int32, (rows, LANES), 0)
    lane = lax.broadcasted_iota(jnp.int32, (rows, LANES), 1)
    cpp = LANES // CMP_STRIDE
    rstride = 2 * NSA_KV
    for g in range(NSA_KV):
        kvc = []
        for kind, pages in enumerate((kp, vp)):
            for p in range(npg):
                for s in range(CMP_STRIDE):
                    xc_ref[p * cpp:(p + 1) * cpp, s * HEAD_DIM:(s + 1) * HEAD_DIM] = (
                        pages[p][0, pl.ds(rstride * s + g, cpp, stride=rstride * CMP_STRIDE), :])
            kvc.append(_compress_core(xc_ref[...].astype(BF), w1_ref, pos_ref, w2_ref, kind))
        qs = [z_ref[0, :, QA0 + (g * NSA_REP + r) * HEAD_DIM:QA0 + (g * NSA_REP + r + 1) * HEAD_DIM].astype(BF)
              for r in range(NSA_REP)]
        outs, sel = _cmp_sel_math(qs, kvc[0], kvc[1], bc_ref[g * NSA_REP:(g + 1) * NSA_REP], tpos, nb, ns,
                                  ovl_ref[...])
        for r in range(NSA_REP):
            hd = g * NSA_REP + r
            oc_ref[0, :, hd * HEAD_DIM:(hd + 1) * HEAD_DIM] = outs[r] * ga[:, hd * 3:hd * 3 + 1]
        qst = jnp.concatenate(qs, axis=0)
        ks0 = KV0 + 2 * NSA_KV * HEAD_DIM + g * HEAD_DIM
        vs0 = KV0 + 3 * NSA_KV * HEAD_DIM + g * HEAD_DIM
        knew = _pad_rows(z_ref[0, :, ks0:ks0 + HEAD_DIM], LANES).astype(BF)
        vnew = _pad_rows(z_ref[0, :, vs0:vs0 + HEAD_DIM], LANES).astype(BF)
        bpp = LANES // SEL_BLOCK
        s_tiles, m_tiles = [], []
        for p in range(npg + 1):
            if p < npg:
                kt = kp[p][0, pl.ds(NSA_KV + g, LANES, stride=rstride), :].astype(BF)
                mk = jnp.zeros((rows, LANES), F32)
                for bq in range(bpp):
                    in_blk = (lane >= bq * SEL_BLOCK) & (lane < (bq + 1) * SEL_BLOCK)
                    mk = jnp.where(in_blk, sel[:, p * bpp + bq:p * bpp + bq + 1], mk)
                mk = mk > 0.5
            else:
                kt = knew
                mk = (lane <= rowi) & (lane < rows) & (sel[:, ns - 1:ns] > 0.5)
            bias = bs_ref[g * NSA_REP:(g + 1) * NSA_REP, :, p * LANES:(p + 1) * LANES].reshape(NSA_REP * rows, LANES)
            mk = jnp.concatenate([mk] * NSA_REP, axis=0)
            s_tiles.append(jnp.where(mk, _nt(qst, kt) * SCALE + bias, NEG_INF))
            m_tiles.append(mk)
        mx = s_tiles[0].max(axis=-1, keepdims=True)
        for st in s_tiles[1:]:
            mx = jnp.maximum(mx, st.max(axis=-1, keepdims=True))
        den = jnp.zeros((NSA_REP * rows, 1), F32)
        acc = jnp.zeros((NSA_REP * rows, HEAD_DIM), F32)
        for p in range(npg + 1):
            pr = jnp.exp(s_tiles[p] - mx) * m_tiles[p].astype(F32)
            den = den + jnp.sum(pr, axis=-1, keepdims=True)
            if p < npg:
                vt = vp[p][0, pl.ds(NSA_KV + g, LANES, stride=rstride), :].astype(BF)
            else:
                vt = vnew
            acc = acc + _dot(pr.astype(BF), vt)
        o = acc / jnp.maximum(den, 1e-30)
        for r in range(NSA_REP):
            hd = g * NSA_REP + r
            os_ref[0, :, hd * HEAD_DIM:(hd + 1) * HEAD_DIM] = o[r * rows:(r + 1) * rows] * ga[:, hd * 3 + 1:hd * 3 + 2]


def _nsa_s(pt, ck, cv, zs, ga, w1, pos, w2, bias_c, bias_s, ovl):
    bsz, npg = pt.shape
    rows = SAMPLE_ROWS
    prow = ck.shape[1]
    page = lambda p: pl.BlockSpec((1, prow, HEAD_DIM), functools.partial(lambda b, t, p: (t[b, p], 0, 0), p=p))
    const = lambda shp: pl.BlockSpec(shp, lambda b, t: (0,) * len(shp))
    in_specs = ([page(p) for p in range(npg)] + [page(p) for p in range(npg)]
                + [pl.BlockSpec((1, rows, PROJ_COLS), lambda b, t: (b, 0, 0)),
                   pl.BlockSpec((1, rows, LANES), lambda b, t: (b, 0, 0)),
                   const(w1.shape), const(pos.shape), const(w2.shape), const(bias_c.shape), const(bias_s.shape),
                   const(ovl.shape)])
    ospec = pl.BlockSpec((1, rows, NSA_WIDTH), lambda b, t: (b, 0, 0))
    gs = pltpu.PrefetchScalarGridSpec(
        num_scalar_prefetch=1, grid=(bsz,), in_specs=in_specs, out_specs=[ospec, ospec],
        scratch_shapes=[pltpu.VMEM((npg * LANES // CMP_STRIDE, CMP_STRIDE * HEAD_DIM), F32)])
    sds = jax.ShapeDtypeStruct((bsz, rows, NSA_WIDTH), F32)
    return pl.pallas_call(
        functools.partial(_nsa_s_body, npg=npg),
        grid_spec=gs, out_shape=[sds, sds],
        compiler_params=_params(("parallel",), 56),
        name="nsa_s",
    )(pt, *([ck] * npg), *([cv] * npg), zs, ga, w1, pos, w2, bias_c, bias_s, ovl)


def _win_s_body(sk_ref, sv_ref, z_ref, ga_ref, bw_ref, o_ref, nk_ref, nv_ref, *, wbuf, nnew):
    rows = SAMPLE_ROWS
    ga = ga_ref[0]
    rowi = lax.broadcasted_iota(jnp.int32, (rows, LANES), 0)
    lane = lax.broadcasted_iota(jnp.int32, (rows, LANES), 1)
    rowk = lax.broadcasted_iota(jnp.int32, (rows, wbuf), 0)
    colk = lax.broadcasted_iota(jnp.int32, (rows, wbuf), 1)
    dist_old = wbuf + rowk - colk
    m_old = jnp.concatenate([(dist_old >= 0) & (dist_old <= WINDOW)] * NSA_REP, axis=0)
    m_new = jnp.concatenate([(lane <= rowi) & (lane < rows) & (rowi - lane <= WINDOW)] * NSA_REP, axis=0)
    for g in range(NSA_KV):
        qst = jnp.concatenate(
            [z_ref[0, :, QA0 + (g * NSA_REP + r) * HEAD_DIM:QA0 + (g * NSA_REP + r + 1) * HEAD_DIM]
             for r in range(NSA_REP)], axis=0).astype(BF)
        kw0 = KV0 + 4 * NSA_KV * HEAD_DIM + g * HEAD_DIM
        vw0 = KV0 + 5 * NSA_KV * HEAD_DIM + g * HEAD_DIM
        kold = sk_ref[0, pl.ds(g, wbuf, stride=NSA_KV), :].astype(BF)
        vold = sv_ref[0, pl.ds(g, wbuf, stride=NSA_KV), :].astype(BF)
        knew = _pad_rows(z_ref[0, :, kw0:kw0 + HEAD_DIM], LANES).astype(BF)
        vnew = _pad_rows(z_ref[0, :, vw0:vw0 + HEAD_DIM], LANES).astype(BF)
        b_old = bw_ref[g * NSA_REP:(g + 1) * NSA_REP, :, 0:wbuf].reshape(NSA_REP * rows, wbuf)
        b_new = bw_ref[g * NSA_REP:(g + 1) * NSA_REP, :, wbuf:wbuf + LANES].reshape(NSA_REP * rows, LANES)
        s_old = jnp.where(m_old, _nt(qst, kold) * SCALE + b_old, NEG_INF)
        s_new = jnp.where(m_new, _nt(qst, knew) * SCALE + b_new, NEG_INF)
        mx = jnp.maximum(s_old.max(axis=-1, keepdims=True), s_new.max(axis=-1, keepdims=True))
        p_old = jnp.exp(s_old - mx) * m_old.astype(F32)
        p_new = jnp.exp(s_new - mx) * m_new.astype(F32)
        den = jnp.sum(p_old, axis=-1, keepdims=True) + jnp.sum(p_new, axis=-1, keepdims=True)
        o = (_dot(p_old.astype(BF), vold) + _dot(p_new.astype(BF), vnew)) / jnp.maximum(den, 1e-30)
        for r in range(NSA_REP):
            hd = g * NSA_REP + r
            o_ref[0, :, hd * HEAD_DIM:(hd + 1) * HEAD_DIM] = o[r * rows:(r + 1) * rows] * ga[:, hd * 3 + 2:hd * 3 + 3]
    keep = (wbuf - nnew) * NSA_KV
    nk_ref[0, 0:keep, :] = sk_ref[0, nnew * NSA_KV:wbuf * NSA_KV, :]
    nv_ref[0, 0:keep, :] = sv_ref[0, nnew * NSA_KV:wbuf * NSA_KV, :]
    for (c0, dst) in ((KV0 + 4 * NSA_KV * HEAD_DIM, nk_ref), (KV0 + 5 * NSA_KV * HEAD_DIM, nv_ref)):
        new = jnp.concatenate(
            [z_ref[0, i:i + 1, c0 + g * HEAD_DIM:c0 + (g + 1) * HEAD_DIM] for i in range(nnew) for g in range(NSA_KV)],
            axis=0)
        dst[0, keep:wbuf * NSA_KV, :] = new


def _win_s(st_k, st_v, layer, zs, ga, bias_w, nnew):
    bsz = zs.shape[0]
    rows = SAMPLE_ROWS
    srows = st_k.shape[1]
    wbuf = srows // NSA_KV
    sspec = pl.BlockSpec((1, srows, HEAD_DIM), lambda b: (layer * bsz + b, 0, 0))
    nspec = pl.BlockSpec((1, srows, HEAD_DIM), lambda b: (b, 0, 0))
    nsds = jax.ShapeDtypeStruct((bsz, srows, HEAD_DIM), F32)
    return pl.pallas_call(
        functools.partial(_win_s_body, wbuf=wbuf, nnew=nnew),
        grid=(bsz,),
        in_specs=[sspec, sspec,
                  pl.BlockSpec((1, rows, PROJ_COLS), lambda b: (b, 0, 0)),
                  pl.BlockSpec((1, rows, LANES), lambda b: (b, 0, 0)),
                  pl.BlockSpec(bias_w.shape, lambda b: (0, 0, 0))],
        out_specs=[pl.BlockSpec((1, rows, NSA_WIDTH), lambda b: (b, 0, 0)), nspec, nspec],
        out_shape=[jax.ShapeDtypeStruct((bsz, rows, NSA_WIDTH), F32), nsds, nsds],
        compiler_params=_params(("parallel",), 32),
        name="win_s",
    )(st_k, st_v, zs, ga, bias_w)


def _fox_s_body(pt_ref, *refs, npg):
    del pt_ref
    kp = refs[:npg]
    vp = refs[npg:2 * npg]
    fp = refs[2 * npg:3 * npg]
    z_ref, lf_ref, o_ref = refs[3 * npg:]
    rows = SAMPLE_ROWS
    rowi = lax.broadcasted_iota(jnp.int32, (rows, LANES), 0)
    lane = lax.broadcasted_iota(jnp.int32, (rows, LANES), 1)
    r128 = lax.broadcasted_iota(jnp.int32, (LANES, LANES), 0)
    c128 = lax.broadcasted_iota(jnp.int32, (LANES, LANES), 1)
    upper = (r128 <= c128).astype(BF)
    ft = []
    carry = jnp.zeros((LANES, 1), F32)
    for p in range(npg):
        lp = fp[p][0]
        lpad = jnp.concatenate([lp, jnp.zeros((LANES, LANES - FOX_HEADS), F32)], axis=1)
        hi, mid, lo = _split3(lpad)
        cs = _tn(hi, upper) + _tn(mid, upper) + _tn(lo, upper) + carry
        carry = cs[:, LANES - 1:LANES]
        ft.append(cs[0:FOX_HEADS])
    lf_new = lf_ref[0]
    x = lf_new
    for sh in (1, 2, 4):
        x = x + jnp.where(rowi >= sh, pltpu.roll(x, sh, 0), 0.0)
    tot_t = jnp.broadcast_to(carry, (LANES, LANES)).T
    tot_row = pltpu.roll(tot_t[0:1, :], FB_LANE, 1)
    f_new = x + tot_row
    f_new_t = _pad_rows(f_new, LANES).T
    m_new = (lane <= rowi) & (lane < rows)
    for h in range(FOX_HEADS):
        q = z_ref[0, :, QB0 + h * HEAD_DIM:QB0 + (h + 1) * HEAD_DIM].astype(BF)
        k0 = QB0 + FOX_WIDTH + h * HEAD_DIM
        v0 = QB0 + 2 * FOX_WIDTH + h * HEAD_DIM
        knew = _pad_rows(z_ref[0, :, k0:k0 + HEAD_DIM], LANES).astype(BF)
        vnew = _pad_rows(z_ref[0, :, v0:v0 + HEAD_DIM], LANES).astype(BF)
        fq = f_new[:, FB_LANE + h:FB_LANE + h + 1]
        s_tiles = []
        for p in range(npg):
            kt = kp[p][0, pl.ds(h, LANES, stride=FOX_HEADS), :].astype(BF)
            s_tiles.append(_nt(q, kt) * SCALE + (fq - ft[p][h:h + 1, :]))
        s_n = _nt(q, knew) * SCALE + (fq - f_new_t[FB_LANE + h:FB_LANE + h + 1, :])
        s_tiles.append(jnp.where(m_new, s_n, NEG_INF))
        mx = s_tiles[0].max(axis=-1, keepdims=True)
        for st in s_tiles[1:]:
            mx = jnp.maximum(mx, st.max(axis=-1, keepdims=True))
        den = jnp.zeros((rows, 1), F32)
        acc = jnp.zeros((rows, HEAD_DIM), F32)
        for p in range(npg + 1):
            pr = jnp.exp(s_tiles[p] - mx)
            if p == npg:
                pr = pr * m_new.astype(F32)
                vt = vnew
            else:
                vt = vp[p][0, pl.ds(h, LANES, stride=FOX_HEADS), :].astype(BF)
            den = den + jnp.sum(pr, axis=-1, keepdims=True)
            acc = acc + _dot(pr.astype(BF), vt)
        o_ref[0, :, h * HEAD_DIM:(h + 1) * HEAD_DIM] = acc / jnp.maximum(den, 1e-30)


def _fox_s(pt, ck, cv, cf, zs, lf):
    bsz, npg = pt.shape
    rows = SAMPLE_ROWS
    prow = ck.shape[1]
    pmap = lambda p: functools.partial(lambda b, t, p: (t[b, p], 0, 0), p=p)
    in_specs = ([pl.BlockSpec((1, prow, HEAD_DIM), pmap(p)) for p in range(npg)]
                + [pl.BlockSpec((1, prow, HEAD_DIM), pmap(p)) for p in range(npg)]
                + [pl.BlockSpec((1, LANES, FOX_HEADS), pmap(p)) for p in range(npg)]
                + [pl.BlockSpec((1, rows, PROJ_COLS), lambda b, t: (b, 0, 0)),
                   pl.BlockSpec((1, rows, LANES), lambda b, t: (b, 0, 0))])
    gs = pltpu.PrefetchScalarGridSpec(
        num_scalar_prefetch=1, grid=(bsz,), in_specs=in_specs,
        out_specs=pl.BlockSpec((1, rows, FOX_WIDTH), lambda b, t: (b, 0, 0)))
    return pl.pallas_call(
        functools.partial(_fox_s_body, npg=npg),
        grid_spec=gs, out_shape=jax.ShapeDtypeStruct((bsz, rows, FOX_WIDTH), F32),
        compiler_params=_params(("parallel",), 56),
        name="fox_s",
    )(pt, *([ck] * npg), *([cv] * npg), *([cf] * npg), zs, lf)


def _rel_bucket(dist):
    n = jnp.maximum(dist, 0)
    exact = REL_BUCKETS // 2
    scaled = jnp.log(jnp.maximum(n, 1).astype(F32) / exact) / math.log(REL_MAX_DIST / exact)
    large = jnp.minimum(exact + (scaled * (REL_BUCKETS - exact)).astype(jnp.int32), REL_BUCKETS - 1)
    return jnp.where(n < exact, n, large)


def _bias_of(rel_table, dist):
    bucket = _rel_bucket(dist)
    out = jnp.zeros((rel_table.shape[1],) + dist.shape, F32)
    for b in range(REL_BUCKETS):
        row = rel_table[b].reshape((-1,) + (1,) * dist.ndim)
        out = jnp.where(bucket[None] == b, row, out)
    return out


def _overlap(nb, ns):
    i = np.arange(LANES)[:, None]
    j = np.arange(LANES)[None, :]
    rs = SEL_BLOCK // CMP_STRIDE
    rc = CMP_LEN // CMP_STRIDE
    m = np.zeros((LANES, LANES), np.float32)
    for a in range(rs):
        for b in range(rc):
            m = m + (i == j * rs + a + b - (rc - 1))
    m = m * (i < nb) * (j < ns)
    return jnp.asarray(m, BF)


def kernel(x_prompt, x_sample, cache_nsa_k, cache_nsa_v, cache_fox_k, cache_fox_v, cache_fox_logf,
           state_nsa_win_k, state_nsa_win_v, page_table, c_prompt, c_sample, w_ada, b_ada, w_in,
           b_forget, cmp_pos, cmp_w1, cmp_w2, rel_table, w_branch_a, w_branch_b, w_out, ln_g, ln_b,
           peer_wq, peer_subkeys, peer_u, peer_v):
    bp, seq, d = x_prompt.shape
    bs, dec = x_sample.shape[:2]
    depth, n_pool, page = cache_nsa_k.shape[:3]
    npg = page_table.shape[1]
    past = npg * page
    wbuf = state_nsa_win_k.shape[2]
    rows = SAMPLE_ROWS
    assert d == D_MODEL and page == LANES and depth == DEPTH and dec <= rows
    assert seq % FLASH_T == 0 or seq < FLASH_T

    ck_nsa = cache_nsa_k.reshape(depth * n_pool, page * 2 * NSA_KV, HEAD_DIM)
    cv_nsa = cache_nsa_v.reshape(depth * n_pool, page * 2 * NSA_KV, HEAD_DIM)
    ck_fox = cache_fox_k.reshape(depth * n_pool, page * FOX_HEADS, HEAD_DIM)
    cv_fox = cache_fox_v.reshape(depth * n_pool, page * FOX_HEADS, HEAD_DIM)
    cf_fox = cache_fox_logf.reshape(depth * n_pool, page, FOX_HEADS)
    st_k = state_nsa_win_k.reshape(depth * bs, wbuf * NSA_KV, HEAD_DIM)
    st_v = state_nsa_win_v.reshape(depth * bs, wbuf * NSA_KV, HEAD_DIM)

    tq = _flash_tile("sel", seq)
    ii = jnp.arange(tq)

    def tile_bias(dd):
        b = _bias_of(rel_table, dd * tq + ii[None, :] - ii[:, None])
        return b.reshape(NSA_KV, NSA_REP, tq, tq).transpose(0, 2, 1, 3).reshape(NSA_KV, tq, NSA_REP * tq)

    bias_tile = jnp.stack([tile_bias(dd) for dd in range(3)])
    nb_p = seq // CMP_STRIDE - 1
    ns_p = -(-seq // SEL_BLOCK)
    lane = jnp.arange(LANES)
    bias_cmp_p = _bias_of(rel_table, jnp.arange(seq)[:, None] - (lane * CMP_STRIDE + CMP_LEN - 1)[None, :])
    nb_s = past // CMP_STRIDE - 1
    ns_s = -(-(past + dec) // SEL_BLOCK)
    tq_s = past + jnp.arange(rows)
    bias_cmp_s = _bias_of(rel_table, tq_s[:, None] - (lane * CMP_STRIDE + CMP_LEN - 1)[None, :])
    bias_sel_s = _bias_of(rel_table, tq_s[:, None] - jnp.arange(past + LANES)[None, :])
    bias_win_s = _bias_of(rel_table, wbuf + jnp.arange(rows)[:, None] - jnp.arange(wbuf + LANES)[None, :])
    ovl_p = _overlap(nb_p, ns_p)
    ovl_s = _overlap(nb_s, ns_s)

    c_all = jnp.concatenate([c_prompt, c_sample], axis=0)
    xp = x_prompt
    xs = x_sample
    outs_p = {k: [] for k in ("nsa_k", "nsa_v", "fox_k", "fox_v", "logf", "win_k", "win_v")}
    outs_s = {k: [] for k in ("nsa_k", "nsa_v", "fox_k", "fox_v", "logf", "win_k", "win_v")}
    gh = NSA_KV * HEAD_DIM

    for l in range(depth):
        wl = w_in[l]
        o_ga = NSA_WIDTH + 6 * gh
        o_qb = o_ga + 3 * NSA_HEADS
        o_fb = o_qb + 3 * FOX_WIDTH
        o_gm = o_fb + FOX_HEADS
        wcat = jnp.concatenate(
            [wl[:, :o_ga], wl[:, o_qb:o_fb], wl[:, o_gm:], wl[:, o_ga:o_qb], wl[:, o_fb:o_gm],
             jnp.zeros((d, PROJ_COLS - GF0 - 3 * NSA_HEADS - FOX_HEADS), F32)], axis=1).astype(BF)
        bfpad = jnp.zeros((1, LANES), F32).at[0, FB_LANE:FB_LANE + FOX_HEADS].set(b_forget[l])
        w1 = cmp_w1[l].astype(BF)
        pos = cmp_pos[l].reshape(2, 1, CMP_LEN * HEAD_DIM)
        w2 = cmp_w2[l].astype(BF)
        wa = w_branch_a[l].astype(BF)
        wb = w_branch_b[l].astype(BF)
        wo = w_out[l].astype(BF)
        wq = peer_wq[l].astype(BF)
        sk = peer_subkeys[l].astype(BF)
        pu = peer_u[l].astype(BF)
        pv = peer_v[l].astype(BF)
        g1, b1 = ln_g[l, 0][None], ln_b[l, 0][None]
        g2, b2 = ln_g[l, 1][None], ln_b[l, 1][None]
        pt_l = page_table + l * n_pool

        mod = _ada(c_all, w_ada[l], b_ada[l][None]).reshape(bp + bs, 6, d)
        mod_p = mod[:bp]
        mod_s = mod[bp:]

        z = _proj(xp, mod_p[:, 0:1], mod_p[:, 1:2], wcat, tm=min(1024, seq), tn=1024)
        ga, lf, fcum = _gates(z, bfpad)
        kvc = _compress_p(z, w1, pos, w2)
        o_cmp, sel = _cmp_p(z, kvc, bias_cmp_p, ga, ovl_p, nb_p, ns_p)
        o_sel = _flash("sel", z, (bias_tile, ga, sel))
        o_win = _flash("win", z, (bias_tile, ga))
        f_t = jnp.swapaxes(fcum[:, :, FB_LANE:FB_LANE + FOX_HEADS], 1, 2)
        o_b = _flash("fox", z, (fcum, f_t))
        nrow = bp * seq
        u = _merge1(o_cmp.reshape(nrow, -1), o_sel.reshape(nrow, -1), o_win.reshape(nrow, -1),
                    o_b.reshape(nrow, -1), z.reshape(nrow, -1), wa, wb)
        x1 = _merge2(u.reshape(bp, seq, d), wo, xp, mod_p[:, 2:3], g1, b1)
        q, h2 = _proj(x1, mod_p[:, 3:4], mod_p[:, 4:5], wq, tm=min(1024, seq), tn=1024, emit_h=True)
        ra, rb, rea, reb, tau = _route(q.reshape(nrow, -1), sk)
        xp = _peer(h2.reshape(nrow, d), pu, pv, ra, rb, rea, reb, tau, x1.reshape(nrow, d),
                   mod_p[:, 5:6], g2, b2, rows_per_gate=seq).reshape(bp, seq, d)
        kv = z[:, :, KV0:KV0 + 6 * gh].reshape(bp, seq, 6, NSA_KV, HEAD_DIM)
        outs_p["nsa_k"].append(kv[:, :, 0:4:2])
        outs_p["nsa_v"].append(kv[:, :, 1:4:2])
        outs_p["fox_k"].append(z[:, :, QB0 + FOX_WIDTH:QB0 + 2 * FOX_WIDTH].reshape(bp, seq, FOX_HEADS, HEAD_DIM))
        outs_p["fox_v"].append(z[:, :, QB0 + 2 * FOX_WIDTH:QB0 + 3 * FOX_WIDTH].reshape(bp, seq, FOX_HEADS, HEAD_DIM))
        outs_p["logf"].append(lf[:, :, FB_LANE:FB_LANE + FOX_HEADS])
        keep = min(WINDOW, seq)
        outs_p["win_k"].append(kv[:, seq - keep:, 4])
        outs_p["win_v"].append(kv[:, seq - keep:, 5])

        xs8 = jnp.pad(xs, ((0, 0), (0, rows - dec), (0, 0)))
        mrow = lambda k: jnp.repeat(mod_s[:, k], rows, axis=0)[None]
        zs = _proj(xs8.reshape(1, bs * rows, d), mrow(0), mrow(1), wcat, tm=min(512, bs * rows), tn=1024)
        ga_s, lf_s, _ = _gates(zs, bfpad)
        zs = zs.reshape(bs, rows, PROJ_COLS)
        ga_s = ga_s.reshape(bs, rows, LANES)
        lf_s = lf_s.reshape(bs, rows, LANES)
        oc_s, os_s = _nsa_s(pt_l, ck_nsa, cv_nsa, zs, ga_s, w1, pos, w2, bias_cmp_s, bias_sel_s, ovl_s)
        ow_s, nk_s, nv_s = _win_s(st_k, st_v, l, zs, ga_s, bias_win_s, dec)
        ob_s = _fox_s(pt_l, ck_fox, cv_fox, cf_fox, zs, lf_s)
        srow = bs * rows
        u_s = _merge1(oc_s.reshape(srow, -1), os_s.reshape(srow, -1), ow_s.reshape(srow, -1),
                      ob_s.reshape(srow, -1), zs.reshape(srow, -1), wa, wb)
        x1_s = _merge2(u_s.reshape(1, srow, d), wo, xs8.reshape(1, srow, d), mrow(2), g1, b1)
        x1_c = x1_s.reshape(bs, rows, d)[:, :dec].reshape(1, bs * dec, d)
        crow = lambda k: jnp.repeat(mod_s[:, k], dec, axis=0)[None]
        q_s, h2_s = _proj(x1_c, crow(3), crow(4), wq, tm=min(512, bs * dec), tn=1024, emit_h=True)
        ra, rb, rea, reb, tau = _route(q_s.reshape(bs * dec, -1), sk)
        xs = _peer(h2_s.reshape(bs * dec, d), pu, pv, ra, rb, rea, reb, tau, x1_c.reshape(bs * dec, d),
                   crow(5).reshape(-1, PEER_TOK, d), g2, b2, rows_per_gate=1).reshape(bs, dec, d)
        kv_s = zs[:, :dec, KV0:KV0 + 6 * gh].reshape(bs, dec, 6, NSA_KV, HEAD_DIM)
        outs_s["nsa_k"].append(kv_s[:, :, 0:4:2])
        outs_s["nsa_v"].append(kv_s[:, :, 1:4:2])
        outs_s["fox_k"].append(zs[:, :dec, QB0 + FOX_WIDTH:QB0 + 2 * FOX_WIDTH].reshape(bs, dec, FOX_HEADS, HEAD_DIM))
        outs_s["fox_v"].append(zs[:, :dec, QB0 + 2 * FOX_WIDTH:QB0 + 3 * FOX_WIDTH].reshape(bs, dec, FOX_HEADS, HEAD_DIM))
        outs_s["logf"].append(lf_s[:, :dec, FB_LANE:FB_LANE + FOX_HEADS])
        outs_s["win_k"].append(nk_s.reshape(bs, wbuf, NSA_KV, HEAD_DIM))
        outs_s["win_v"].append(nv_s.reshape(bs, wbuf, NSA_KV, HEAD_DIM))

    st = lambda dct, k: jnp.stack(dct[k])
    return (xp, xs,
            st(outs_p, "nsa_k"), st(outs_p, "nsa_v"), st(outs_p, "fox_k"), st(outs_p, "fox_v"),
            st(outs_p, "logf"), st(outs_p, "win_k"), st(outs_p, "win_v"),
            st(outs_s, "nsa_k"), st(outs_s, "nsa_v"), st(outs_s, "fox_k"), st(outs_s, "fox_v"),
            st(outs_s, "logf"), st(outs_s, "win_k"), st(outs_s, "win_v"))
```

```python
import functools
import math

import numpy as np
import jax
import jax.numpy as jnp
from jax import lax
from jax.experimental import pallas as pl
from jax.experimental.pallas import tpu as pltpu

F32 = jnp.float32
BF = jnp.bfloat16

D_MODEL = 2048
DEPTH = 2
HEAD_DIM = 128
NSA_HEADS = D_MODEL // (2 * HEAD_DIM)
NSA_KV = max(1, NSA_HEADS // 4)
NSA_REP = NSA_HEADS // NSA_KV
FOX_HEADS = D_MODEL // (2 * HEAD_DIM)
NSA_WIDTH = NSA_HEADS * HEAD_DIM
FOX_WIDTH = FOX_HEADS * HEAD_DIM
CMP_LEN = 32
CMP_STRIDE = 16
CMP_HID = 256
SEL_BLOCK = 64
SEL_TOPN = 16
WINDOW = 512
REL_BUCKETS = 32
REL_MAX_DIST = 128
PEER_HEADS = 8
PEER_NKEYS = 128
PEER_EXPERTS = PEER_NKEYS * PEER_NKEYS
PEER_DK = 256
PEER_TOPK = 16
DEEPNORM_ALPHA = (2 * DEPTH) ** 0.25
LN_EPS = 1e-5
NEG_INF = -1e9
FORCE_SCORE = 1e9
SCALE = HEAD_DIM ** -0.5

LANES = 128
SUB = 8
VMEM_MB = 1024 * 1024

QA0 = 0
KV0 = QA0 + NSA_WIDTH
QB0 = KV0 + 6 * NSA_KV * HEAD_DIM
GM0 = QB0 + 3 * FOX_WIDTH
GF0 = GM0 + 2 * D_MODEL
PROJ_COLS = 10240
FB_LANE = 3 * NSA_HEADS
SAMPLE_ROWS = 8
PEER_TOK = 512
PEER_ETILE = 512
FLASH_T = 256
FOX_T = 512


def _nt(a, b):
    return lax.dot_general(a, b, (((1,), (1,)), ((), ())), preferred_element_type=F32)


def _tn(a, b):
    return lax.dot_general(a, b, (((0,), (0,)), ((), ())), preferred_element_type=F32)


def _dot(a, b):
    return jnp.dot(a, b, preferred_element_type=F32)


def _split3(x):
    hi = x.astype(BF)
    r1 = x - hi.astype(F32)
    mid = r1.astype(BF)
    lo = (r1 - mid.astype(F32)).astype(BF)
    return hi, mid, lo


def _params(sem, mb):
    return pltpu.CompilerParams(dimension_semantics=sem, vmem_limit_bytes=mb * VMEM_MB)


def _lane_col(x, idx):
    lane = lax.broadcasted_iota(jnp.int32, x.shape, 1)
    return jnp.sum(jnp.where(lane == idx, x, 0.0), axis=-1, keepdims=True)


def _ada_body(c_ref, w_ref, b_ref, o_ref):
    c = c_ref[...]
    a = (c * jax.nn.sigmoid(c)).astype(BF)
    o_ref[...] = _dot(a, w_ref[...].astype(BF)) + b_ref[...]


def _ada(c, w, b):
    m, d = c.shape
    n = w.shape[1]
    tn = 1536
    return pl.pallas_call(
        _ada_body,
        grid=(n // tn,),
        in_specs=[pl.BlockSpec((m, d), lambda j: (0, 0)),
                  pl.BlockSpec((d, tn), lambda j: (0, j)),
                  pl.BlockSpec((1, tn), lambda j: (0, j))],
        out_specs=pl.BlockSpec((m, tn), lambda j: (0, j)),
        out_shape=jax.ShapeDtypeStruct((m, n), F32),
        compiler_params=_params(("parallel",), 48),
        name="ada",
    )(c, w, b)


def _proj_body(x_ref, sh_ref, sc_ref, w_ref, o_ref, *rest, emit_h):
    if emit_h:
        hb_ref, h_ref = rest
    else:
        (h_ref,) = rest

    @pl.when(pl.program_id(2) == 0)
    def _():
        h = (x_ref[0] * (1.0 + sc_ref[0]) + sh_ref[0]).astype(BF)
        h_ref[...] = h
        if emit_h:
            hb_ref[0] = h

    o_ref[0] = _dot(h_ref[...], w_ref[...])


def _proj(x, shift, scale, w, tm, tn, emit_h=False):
    n, t, d = x.shape
    c = w.shape[1]
    ts = shift.shape[1]
    tms = tm if ts == t else 1
    mod_map = (lambda b, i, j: (b, i, 0)) if ts == t else (lambda b, i, j: (b, 0, 0))
    out_shape = [jax.ShapeDtypeStruct((n, t, c), F32)]
    out_specs = [pl.BlockSpec((1, tm, tn), lambda b, i, j: (b, i, j))]
    if emit_h:
        out_shape.append(jax.ShapeDtypeStruct((n, t, d), BF))
        out_specs.append(pl.BlockSpec((1, tm, d), lambda b, i, j: (b, i, 0)))
    res = pl.pallas_call(
        functools.partial(_proj_body, emit_h=emit_h),
        grid=(n, t // tm, c // tn),
        in_specs=[pl.BlockSpec((1, tm, d), lambda b, i, j: (b, i, 0)),
                  pl.BlockSpec((1, tms, d), mod_map),
                  pl.BlockSpec((1, tms, d), mod_map),
                  pl.BlockSpec((d, tn), lambda b, i, j: (0, j))],
        out_specs=out_specs,
        out_shape=out_shape,
        scratch_shapes=[pltpu.VMEM((tm, d), BF)],
        compiler_params=_params(("parallel", "parallel", "arbitrary"), 56),
        name="proj",
    )(x, shift, scale, w)
    return res if emit_h else res[0]


def _cumsum_rows(x, tri):
    hi, mid, lo = _split3(x)
    return _dot(tri, hi) + _dot(tri, mid) + _dot(tri, lo)


def _gates_body(z_ref, bf_ref, ga_ref, lf_ref, f_ref, *, t, blk):
    z = z_ref[0]
    ga_ref[0] = jax.nn.sigmoid(z)
    lf = jax.nn.log_sigmoid(z + bf_ref[...])
    lf_ref[0] = lf
    r = lax.broadcasted_iota(jnp.int32, (blk, blk), 0)
    c = lax.broadcasted_iota(jnp.int32, (blk, blk), 1)
    tri = (r >= c).astype(BF)
    carry = jnp.zeros((1, LANES), F32)
    for b in range(t // blk):
        fb = _cumsum_rows(lf[b * blk:(b + 1) * blk], tri) + carry
        f_ref[0, b * blk:(b + 1) * blk, :] = fb
        carry = fb[blk - 1:blk, :]


def _gates(z, bfpad):
    n, t, _ = z.shape
    blk = min(256, t)
    sds = jax.ShapeDtypeStruct((n, t, LANES), F32)
    spec = pl.BlockSpec((1, t, LANES), lambda b: (b, 0, 0))
    return pl.pallas_call(
        functools.partial(_gates_body, t=t, blk=blk),
        grid=(n,),
        in_specs=[pl.BlockSpec((1, t, LANES), lambda b: (b, 0, GF0 // LANES)),
                  pl.BlockSpec((1, LANES), lambda b: (0, 0))],
        out_specs=[spec, spec, spec],
        out_shape=[sds, sds, sds],
        compiler_params=_params(("parallel",), 32),
        name="gates",
    )(z, bfpad)


def _compress_core(xc, w1_ref, pos_ref, w2_ref, kind):
    m = xc.shape[0]
    half = CMP_STRIDE * HEAD_DIM
    first = _dot(xc, w1_ref[kind, 0:half, :])
    second = _dot(xc, w1_ref[kind, half:2 * half, :])
    posb = jnp.broadcast_to(pos_ref[kind], (SUB, 2 * half)).astype(BF)
    pos_term = _dot(posb, w1_ref[kind])[0:1]
    hid = jax.nn.gelu(first + pltpu.roll(second, m - 1, 0) + pos_term)
    return _dot(hid.astype(BF), w2_ref[kind])


def _compress_p_body(x_ref, w1_ref, pos_ref, w2_ref, o_ref, xc_ref, *, m):
    kind = pl.program_id(1) // NSA_KV
    for s in range(CMP_STRIDE):
        xc_ref[:, s * HEAD_DIM:(s + 1) * HEAD_DIM] = x_ref[0, pl.ds(s, m, stride=CMP_STRIDE), :].astype(BF)
    o_ref[0, 0] = _compress_core(xc_ref[...], w1_ref, pos_ref, w2_ref, kind)


def _compress_p(z, w1, pos, w2):
    n, t, _ = z.shape
    m = t // CMP_STRIDE
    return pl.pallas_call(
        functools.partial(_compress_p_body, m=m),
        grid=(n, 2 * NSA_KV),
        in_specs=[pl.BlockSpec((1, t, HEAD_DIM), lambda b, a: (b, 0, KV0 // HEAD_DIM + a)),
                  pl.BlockSpec(w1.shape, lambda b, a: (0, 0, 0)),
                  pl.BlockSpec(pos.shape, lambda b, a: (0, 0, 0)),
                  pl.BlockSpec(w2.shape, lambda b, a: (0, 0, 0))],
        out_specs=pl.BlockSpec((1, 1, m, HEAD_DIM), lambda b, a: (b, a, 0, 0)),
        out_shape=jax.ShapeDtypeStruct((n, 2 * NSA_KV, m, HEAD_DIM), F32),
        scratch_shapes=[pltpu.VMEM((m, CMP_STRIDE * HEAD_DIM), BF)],
        compiler_params=_params(("parallel", "parallel"), 32),
        name="compress_p",
    )(z, w1, pos, w2)


def _masked_softmax(s, mask):
    s = jnp.where(mask, s, NEG_INF)
    p = jnp.exp(s - jnp.max(s, axis=-1, keepdims=True)) * mask.astype(F32)
    return p / jnp.maximum(jnp.sum(p, axis=-1, keepdims=True), 1e-30)


def _cmp_sel_math(qs, kc, vc, bias, tpos, nb, ns, ovl, blocks_on_rows=False, tpos_row=None):
    tq = qs[0].shape[0]
    lane = lax.broadcasted_iota(jnp.int32, (tq, LANES), 1)
    dist = tpos - (lane * CMP_STRIDE + (CMP_LEN - 1))
    mask = (dist >= 0) & (lane < nb)
    kcb = kc.astype(BF)
    vcb = vc.astype(BF)
    outs = []
    psum = jnp.zeros((tq, LANES), F32)
    for r in range(NSA_REP):
        p = _masked_softmax(_nt(qs[r], kcb) * SCALE + bias[r], mask)
        outs.append(_dot(p.astype(BF), vcb))
        psum = psum + p
    hi, mid, lo = _split3(psum)
    if blocks_on_rows:
        nr = -(-ns // SUB) * SUB
        imp = (_nt(ovl, hi) + _nt(ovl, mid) + _nt(ovl, lo))[0:nr]
        blk = lax.broadcasted_iota(jnp.int32, (nr, tq), 0)
        trow = tpos_row
        cur = trow // SEL_BLOCK
        forced = (blk == 0) | (blk == cur) | (blk == cur - 1)
        score = jnp.where(forced, FORCE_SCORE, jnp.where(blk * SEL_BLOCK <= trow, imp, NEG_INF))
        score = jnp.where(blk < ns, score, -3.0e38)
        rank = jnp.zeros((nr, tq), F32)
        for j in range(ns):
            rj = score[j:j + 1, :]
            rank = rank + ((rj > score) | ((rj == score) & (blk > j))).astype(F32)
        sel = ((rank < float(min(SEL_TOPN, ns))) & (blk < ns)).astype(F32)
        return outs, jnp.concatenate([sel, jnp.zeros((LANES - nr, tq), F32)], axis=0)
    imp = _dot(hi, ovl) + _dot(mid, ovl) + _dot(lo, ovl)
    cur = tpos // SEL_BLOCK
    valid = lane * SEL_BLOCK <= tpos
    forced = (lane == 0) | (lane == cur) | (lane == cur - 1)
    score = jnp.where(forced, FORCE_SCORE, jnp.where(valid, imp, NEG_INF))
    score = jnp.where(lane < ns, score, -3.0e38)
    rank = jnp.zeros((tq, LANES), F32)
    for j in range(ns):
        cj = score[:, j:j + 1]
        ahead = (cj > score) | ((cj == score) & (lane > j))
        rank = rank + ahead.astype(F32)
    sel = (rank < float(min(SEL_TOPN, ns))) & (lane < ns)
    return outs, sel.astype(F32)


def _cmp_p_body(q_ref, kc_ref, vc_ref, bias_ref, ga_ref, ovl_ref, o_ref, sel_ref, *, tq, nb, ns):
    g = pl.program_id(1)
    qi = pl.program_id(2)
    q = q_ref[0]
    qs = [q[:, r * HEAD_DIM:(r + 1) * HEAD_DIM].astype(BF) for r in range(NSA_REP)]
    tpos = qi * tq + lax.broadcasted_iota(jnp.int32, (tq, 1), 0)
    tpos_row = qi * tq + lax.broadcasted_iota(jnp.int32, (1, tq), 1)
    outs, sel_t = _cmp_sel_math(qs, kc_ref[0, 0], vc_ref[0, 0], bias_ref[...], tpos, nb, ns, ovl_ref[...],
                                blocks_on_rows=True, tpos_row=tpos_row)
    ga = ga_ref[0]
    for r in range(NSA_REP):
        gate = _lane_col(ga, (g * NSA_REP + r) * 3 + 0)
        o_ref[0, :, r * HEAD_DIM:(r + 1) * HEAD_DIM] = outs[r] * gate
    sel_ref[0, 0] = sel_t


def _cmp_p(z, kvc, bias_cmp, ga, ovl, nb, ns):
    n, t, _ = z.shape
    tq = min(FLASH_T, t)
    gw = NSA_REP * HEAD_DIM
    m = kvc.shape[2]
    return pl.pallas_call(
        functools.partial(_cmp_p_body, tq=tq, nb=nb, ns=ns),
        grid=(n, NSA_KV, t // tq),
        in_specs=[pl.BlockSpec((1, tq, gw), lambda b, g, i: (b, i, g)),
                  pl.BlockSpec((1, 1, m, HEAD_DIM), lambda b, g, i: (b, g, 0, 0)),
                  pl.BlockSpec((1, 1, m, HEAD_DIM), lambda b, g, i: (b, NSA_KV + g, 0, 0)),
                  pl.BlockSpec((NSA_REP, tq, LANES), lambda b, g, i: (g, i, 0)),
                  pl.BlockSpec((1, tq, LANES), lambda b, g, i: (b, i, 0)),
                  pl.BlockSpec((LANES, LANES), lambda b, g, i: (0, 0))],
        out_specs=[pl.BlockSpec((1, tq, gw), lambda b, g, i: (b, i, g)),
                   pl.BlockSpec((1, 1, LANES, tq), lambda b, g, i: (b, g, 0, i))],
        out_shape=[jax.ShapeDtypeStruct((n, t, NSA_WIDTH), F32),
                   jax.ShapeDtypeStruct((n, NSA_KV, LANES, t), F32)],
        compiler_params=_params(("parallel", "parallel", "parallel"), 32),
        name="cmp_p",
    )(z, kvc, kvc, bias_cmp, ga, ovl)


def _flash_body(tab_ref, *refs, mode, tq):
    if mode == "sel":
        q_ref, k_ref, v_ref, bias_ref, ga_ref, sel_ref, o_ref, m_ref, l_ref, acc_ref = refs
    elif mode == "win":
        q_ref, k_ref, v_ref, bias_ref, ga_ref, o_ref, m_ref, l_ref, acc_ref = refs
    else:
        q_ref, k_ref, v_ref, fcol_ref, frow_ref, o_ref, m_ref, l_ref, acc_ref = refs
    rep = 1 if mode == "fox" else NSA_REP
    hg = pl.program_id(1)
    pair = pl.program_id(2)
    qi = tab_ref[0, pair]
    tile = tab_ref[1, pair]

    @pl.when(tab_ref[2, pair] == 1)
    def _():
        m_ref[...] = jnp.full(m_ref.shape, NEG_INF, F32)
        l_ref[...] = jnp.zeros(l_ref.shape, F32)
        acc_ref[...] = jnp.zeros(acc_ref.shape, F32)

    q = q_ref[0]
    if rep == 1:
        qs = q.astype(BF)
    else:
        qs = jnp.concatenate([q[:, r * HEAD_DIM:(r + 1) * HEAD_DIM] for r in range(rep)], axis=0).astype(BF)
    s = _nt(k_ref[0].astype(BF), qs) * SCALE
    def causal():
        ki = lax.broadcasted_iota(jnp.int32, (tq, tq), 0)
        qj = lax.broadcasted_iota(jnp.int32, (tq, tq), 1)
        return (qi - tile) * tq + qj - ki

    def update(s, mask):
        if mask is not None:
            s = jnp.where(mask, s, NEG_INF)
        m_old = m_ref[...]
        m_new = jnp.maximum(m_old, jnp.max(s, axis=0, keepdims=True))
        p = jnp.exp(s - m_new)
        if mask is not None:
            p = p * mask.astype(F32)
        alpha = jnp.exp(m_old - m_new)
        l_ref[...] = l_ref[...] * alpha + jnp.sum(p, axis=0, keepdims=True)
        acc_ref[...] = acc_ref[...] * alpha + _tn(v_ref[0].astype(BF), p.astype(BF))
        m_ref[...] = m_new

    if mode == "fox":
        fk = _lane_col(fcol_ref[0], FB_LANE + hg)
        s = s + (frow_ref[0, pl.ds(hg, 1), :] - fk)

        @pl.when(tile == qi)
        def _():
            update(s, causal() >= 0)

        @pl.when(tile != qi)
        def _():
            update(s, None)
    elif mode == "win":
        s = s + bias_ref[0, 0]
        inner = (tile < qi) & ((qi - tile + 1) * tq - 1 <= WINDOW)

        @pl.when(inner)
        def _():
            update(s, None)

        @pl.when(jnp.logical_not(inner))
        def _():
            dist = causal()
            update(s, jnp.concatenate([(dist >= 0) & (dist <= WINDOW)] * rep, axis=1))
    else:
        dist = causal()
        key_blk = (tile * tq + lax.broadcasted_iota(jnp.int32, (tq, LANES), 0)) // SEL_BLOCK
        expand = (key_blk == lax.broadcasted_iota(jnp.int32, (tq, LANES), 1)).astype(BF)
        hit = _dot(expand, sel_ref[0, 0].astype(BF))
        mask = (dist >= 0) & (hit > 0.5)
        update(s + bias_ref[0, 0], jnp.concatenate([mask] * rep, axis=1))

    @pl.when(tab_ref[3, pair] == 1)
    def _():
        o = acc_ref[...] / jnp.maximum(l_ref[...], 1e-30)
        if mode == "fox":
            o_ref[0] = o.T
        else:
            ga = ga_ref[0]
            branch = 1 if mode == "sel" else 2
            for r in range(rep):
                gate = _lane_col(ga, (hg * NSA_REP + r) * 3 + branch)
                o_ref[0, :, r * HEAD_DIM:(r + 1) * HEAD_DIM] = o[:, r * tq:(r + 1) * tq].T * gate


def _flash_tile(mode, t):
    return min(FOX_T if mode == "fox" else FLASH_T, t)


def _flash(mode, z, aux):
    n, t, _ = z.shape
    tq = _flash_tile(mode, t)
    nq = t // tq
    back = WINDOW // tq if mode == "win" else nq
    pairs = [(i, k) for i in range(nq) for k in range(max(i - back, 0), i + 1)]
    tab = jnp.asarray(np.array(
        [[i for i, _ in pairs], [k for _, k in pairs],
         [int(k == max(i - back, 0)) for i, k in pairs], [int(k == i) for i, k in pairs]], np.int32))
    if mode == "fox":
        heads, rep = FOX_HEADS, 1
        qc, kc, vc = QB0 // HEAD_DIM, (QB0 + FOX_WIDTH) // HEAD_DIM, (QB0 + 2 * FOX_WIDTH) // HEAD_DIM
    else:
        heads, rep = NSA_KV, NSA_REP
        off = 2 if mode == "sel" else 4
        qc = QA0 // (rep * HEAD_DIM)
        kc = KV0 // HEAD_DIM + off * NSA_KV
        vc = KV0 // HEAD_DIM + (off + 1) * NSA_KV
    w = rep * HEAD_DIM
    in_specs = [pl.BlockSpec((1, tq, w), lambda b, h, p, tb: (b, tb[0, p], qc + h)),
                pl.BlockSpec((1, tq, HEAD_DIM), lambda b, h, p, tb: (b, tb[1, p], kc + h)),
                pl.BlockSpec((1, tq, HEAD_DIM), lambda b, h, p, tb: (b, tb[1, p], vc + h))]
    if mode == "fox":
        f, ft = aux
        args = (z, z, z, f, ft)
        in_specs += [pl.BlockSpec((1, tq, LANES), lambda b, h, p, tb: (b, tb[1, p], 0)),
                     pl.BlockSpec((1, FOX_HEADS, tq), lambda b, h, p, tb: (b, 0, tb[0, p]))]
    else:
        bias, ga = aux[0], aux[1]
        args = (z, z, z, bias, ga)
        in_specs += [pl.BlockSpec((1, 1, tq, rep * tq),
                                  lambda b, h, p, tb: (jnp.minimum(tb[0, p] - tb[1, p], 2), h, 0, 0)),
                     pl.BlockSpec((1, tq, LANES), lambda b, h, p, tb: (b, tb[0, p], 0))]
        if mode == "sel":
            args += (aux[2],)
            in_specs += [pl.BlockSpec((1, 1, LANES, tq), lambda b, h, p, tb: (b, h, 0, tb[0, p]))]
    gs = pltpu.PrefetchScalarGridSpec(
        num_scalar_prefetch=1, grid=(n, heads, len(pairs)), in_specs=in_specs,
        out_specs=pl.BlockSpec((1, tq, w), lambda b, h, p, tb: (b, tb[0, p], h)),
        scratch_shapes=[pltpu.VMEM((1, rep * tq), F32), pltpu.VMEM((1, rep * tq), F32),
                        pltpu.VMEM((HEAD_DIM, rep * tq), F32)])
    return pl.pallas_call(
        functools.partial(_flash_body, mode=mode, tq=tq),
        grid_spec=gs,
        out_shape=jax.ShapeDtypeStruct((n, t, heads * w), F32),
        compiler_params=_params(("parallel", "parallel", "arbitrary"), 48),
        name="flash_" + mode,
    )(tab, *args)


def _merge1_body(oc_ref, os_ref, ow_ref, ob_ref, ga_ref, gb_ref, wa_ref, wb_ref, u_ref, oa_s, ob_s):
    @pl.when(pl.program_id(1) == 0)
    def _():
        oa_s[...] = (oc_ref[...] + os_ref[...] + ow_ref[...]).astype(BF)
        ob_s[...] = ob_ref[...].astype(BF)

    ya = _dot(oa_s[...], wa_ref[...])
    yb = _dot(ob_s[...], wb_ref[...])
    u_ref[...] = (jax.nn.sigmoid(ga_ref[...]) * ya + jax.nn.sigmoid(gb_ref[...]) * yb).astype(BF)


def _merge1(oc, osel, ow, ob, z2d, wa, wb):
    rows = oc.shape[0]
    tm, tn = 512, 512
    ospec = pl.BlockSpec((tm, NSA_WIDTH), lambda i, j: (i, 0))
    return pl.pallas_call(
        _merge1_body,
        grid=(rows // tm, D_MODEL // tn),
        in_specs=[ospec, ospec, ospec, ospec,
                  pl.BlockSpec((tm, tn), lambda i, j: (i, GM0 // tn + j)),
                  pl.BlockSpec((tm, tn), lambda i, j: (i, (GM0 + D_MODEL) // tn + j)),
                  pl.BlockSpec((NSA_WIDTH, tn), lambda i, j: (0, j)),
                  pl.BlockSpec((FOX_WIDTH, tn), lambda i, j: (0, j))],
        out_specs=pl.BlockSpec((tm, tn), lambda i, j: (i, j)),
        out_shape=jax.ShapeDtypeStruct((rows, D_MODEL), BF),
        scratch_shapes=[pltpu.VMEM((tm, NSA_WIDTH), BF), pltpu.VMEM((tm, FOX_WIDTH), BF)],
        compiler_params=_params(("parallel", "arbitrary"), 48),
        name="merge1",
    )(oc, osel, ow, ob, z2d, z2d, wa, wb)


def _layer_norm(y, g, b):
    mu = jnp.mean(y, axis=-1, keepdims=True)
    yc = y - mu
    var = jnp.mean(yc * yc, axis=-1, keepdims=True)
    return yc * lax.rsqrt(var + LN_EPS) * g + b


def _merge2_body(u_ref, wo_ref, x_ref, gt_ref, g_ref, b_ref, o_ref):
    mix = _dot(u_ref[0], wo_ref[...])
    y = DEEPNORM_ALPHA * x_ref[0] + gt_ref[0] * mix
    o_ref[0] = _layer_norm(y, g_ref[...], b_ref[...])


def _merge2(u, wo, x, gate, g, b):
    n, t, d = x.shape
    tm = min(512, t)
    ts = gate.shape[1]
    gspec = (pl.BlockSpec((1, tm, d), lambda bb, i: (bb, i, 0)) if ts == t
             else pl.BlockSpec((1, 1, d), lambda bb, i: (bb, 0, 0)))
    vec = pl.BlockSpec((1, d), lambda bb, i: (0, 0))
    return pl.pallas_call(
        _merge2_body,
        grid=(n, t // tm),
        in_specs=[pl.BlockSpec((1, tm, d), lambda bb, i: (bb, i, 0)),
                  pl.BlockSpec((d, d), lambda bb, i: (0, 0)),
                  pl.BlockSpec((1, tm, d), lambda bb, i: (bb, i, 0)),
                  gspec, vec, vec],
        out_specs=pl.BlockSpec((1, tm, d), lambda bb, i: (bb, i, 0)),
        out_shape=jax.ShapeDtypeStruct((n, t, d), F32),
        compiler_params=_params(("parallel", "parallel"), 56),
        name="merge2",
    )(u, wo, x, gate, g, b)


def _merge_desc(lst):
    n = len(lst)
    j = n // 2
    while j >= 1:
        for i in range(n):
            l = i ^ j
            if l > i:
                lst[i], lst[l] = jnp.maximum(lst[i], lst[l]), jnp.minimum(lst[i], lst[l])
        j //= 2
    return lst


def _sort_desc(lst):
    n = len(lst)
    k = 2
    while k <= n:
        j = k // 2
        while j >= 1:
            for i in range(n):
                l = i ^ j
                if l > i:
                    hi, lo = jnp.maximum(lst[i], lst[l]), jnp.minimum(lst[i], lst[l])
                    lst[i], lst[l] = (hi, lo) if (i & k) == 0 else (lo, hi)
            j //= 2
        k *= 2
    return lst


def _top_across_sublanes(lst):
    n = len(lst)
    for sh in (SUB // 2, SUB // 4, SUB // 8):
        part = [pltpu.roll(x, sh, 0) for x in lst]
        lst = _merge_desc([jnp.maximum(lst[i], part[n - 1 - i]) for i in range(n)])
    return lst


def _route_body(q_ref, sk_ref, a_ref, b_ref, ea_ref, eb_ref, tau_ref, *, tt):
    half = PEER_DK // 2
    k = PEER_TOPK
    groups = PEER_NKEYS // SUB

    def head(h, tau_all):
        r0 = pl.multiple_of(h * PEER_NKEYS, PEER_NKEYS)
        c0 = pl.multiple_of(h * PEER_DK, PEER_DK)
        a_ref[0, pl.ds(r0, PEER_NKEYS), :] = _nt(sk_ref[h, 0], q_ref[:, pl.ds(c0, half)].astype(BF))
        b_ref[0, pl.ds(r0, PEER_NKEYS), :] = _nt(sk_ref[h, 1], q_ref[:, pl.ds(c0 + half, half)].astype(BF))
        sub = lax.broadcasted_iota(jnp.int32, (SUB, LANES), 0)
        taus = []
        for c in range(tt // LANES):
            lanes = slice(c * LANES, (c + 1) * LANES)
            tops = []
            for ref in (a_ref, b_ref):
                rows = [ref[0, pl.ds(r0 + SUB * r, SUB), lanes] for r in range(groups)]
                tops.append(_top_across_sublanes(_sort_desc(rows)))
            t1, t2 = tops
            best = None
            for kk in range(k // SUB):
                spread = t2[SUB * kk]
                for s in range(1, SUB):
                    spread = jnp.where(sub == s, t2[SUB * kk + s], spread)
                cand = [t1[r] + spread for r in range(k)]
                best = cand if best is None else _merge_desc([jnp.maximum(best[i], cand[k - 1 - i]) for i in range(k)])
            tv = _top_across_sublanes(best)
            zsum = jnp.exp(tv[0] - tv[0])
            for i in range(1, k):
                zsum = zsum + jnp.exp(tv[i] - tv[0])
            s1 = a_ref[0, pl.ds(r0, PEER_NKEYS), lanes]
            s2 = b_ref[0, pl.ds(r0, PEER_NKEYS), lanes]
            ea_ref[0, pl.ds(r0, PEER_NKEYS), lanes] = jnp.exp(s1 - t1[0][0:1]) / zsum[0:1]
            eb_ref[0, pl.ds(r0, PEER_NKEYS), lanes] = jnp.exp(s2 - t2[0][0:1])
            taus.append(tv[k - 1])
        head_row = lax.broadcasted_iota(jnp.int32, (PEER_HEADS, tt), 0) == h
        return jnp.where(head_row, jnp.concatenate(taus, axis=1), tau_all)

    tau_ref[0] = lax.fori_loop(0, PEER_HEADS, head, jnp.zeros((PEER_HEADS, tt), F32))


def _route(q2d, sk):
    rows = q2d.shape[0]
    tt = PEER_TOK
    nblk = rows // tt
    big = jax.ShapeDtypeStruct((nblk, PEER_HEADS * PEER_NKEYS, tt), F32)
    bspec = pl.BlockSpec((1, PEER_HEADS * PEER_NKEYS, tt), lambda i: (i, 0, 0))
    return pl.pallas_call(
        functools.partial(_route_body, tt=tt),
        grid=(nblk,),
        in_specs=[pl.BlockSpec((tt, PEER_HEADS * PEER_DK), lambda i: (i, 0)),
                  pl.BlockSpec(sk.shape, lambda i: (0, 0, 0, 0))],
        out_specs=[bspec, bspec, bspec, bspec, pl.BlockSpec((1, PEER_HEADS, tt), lambda i: (i, 0, 0))],
        out_shape=[big, big, big, big, jax.ShapeDtypeStruct((nblk, PEER_HEADS, tt), F32)],
        compiler_params=_params(("parallel",), 48),
        name="route",
    )(q2d, sk)


def _peer_body(h_ref, u_ref, v_ref, a_ref, b_ref, ea_ref, eb_ref, tau_ref, x_ref, gt_ref, g_ref, bb_ref,
               o_ref, acc_ref, pre_ref, act_ref, *, te, tt):
    j = pl.program_id(1)
    nj = pl.num_programs(1) - 1
    slab = 32
    cur = j % 2

    @pl.when(j == 0)
    def _():
        acc_ref[...] = jnp.zeros(acc_ref.shape, F32)
        act_ref[1] = jnp.zeros(act_ref.shape[1:], BF)

    acc_ref[...] += _tn(act_ref[1 - cur], v_ref[...])

    @pl.when(j < nj)
    def _():
        pre_ref[...] = _nt(u_ref[...], h_ref[...])
        for sidx in range(te // slab):
            a = j * (te // PEER_NKEYS) + (sidx * slab) // PEER_NKEYS
            b0 = (sidx * slab) % PEER_NKEYS
            gm = None
            for h in range(PEER_HEADS):
                arow = a_ref[0, pl.ds(h * PEER_NKEYS + a, 1), :]
                earow = ea_ref[0, pl.ds(h * PEER_NKEYS + a, 1), :]
                bs = b_ref[0, h * PEER_NKEYS + b0:h * PEER_NKEYS + b0 + slab, :]
                ebs = eb_ref[0, h * PEER_NKEYS + b0:h * PEER_NKEYS + b0 + slab, :]
                term = jnp.where(arow + bs >= tau_ref[0, h:h + 1, :], earow * ebs, 0.0)
                gm = term if gm is None else gm + term
            pre = pre_ref[sidx * slab:(sidx + 1) * slab, :]
            act_ref[cur, sidx * slab:(sidx + 1) * slab, :] = (jax.nn.gelu(pre) * gm).astype(BF)

    @pl.when(j == nj)
    def _():
        y = DEEPNORM_ALPHA * x_ref[...] + gt_ref[0] * acc_ref[...]
        o_ref[...] = _layer_norm(y, g_ref[...], bb_ref[...])


def _peer(h2, u, v, a, b, ea, eb, tau, x1, gate, g, bb, rows_per_gate):
    rows, d = x1.shape
    tt, te = PEER_TOK, PEER_ETILE
    nblk = rows // tt
    nj = PEER_EXPERTS // te
    one = pl.Buffered(1)
    rspec = pl.BlockSpec((1, PEER_HEADS * PEER_NKEYS, tt), lambda i, j: (i, 0, 0), pipeline_mode=one)
    if rows_per_gate == 1:
        gspec = pl.BlockSpec((1, tt, d), lambda i, j: (i, 0, 0), pipeline_mode=one)
    else:
        gspec = pl.BlockSpec((1, 1, d), lambda i, j: (i * tt // rows_per_gate, 0, 0))
    vec = pl.BlockSpec((1, d), lambda i, j: (0, 0))
    return pl.pallas_call(
        functools.partial(_peer_body, te=te, tt=tt),
        grid=(nblk, nj + 1),
        in_specs=[pl.BlockSpec((tt, d), lambda i, j: (i, 0), pipeline_mode=one),
                  pl.BlockSpec((te, d), lambda i, j: (jnp.minimum(j, nj - 1), 0)),
                  pl.BlockSpec((te, d), lambda i, j: (jnp.maximum(j - 1, 0), 0)),
                  rspec, rspec, rspec, rspec,
                  pl.BlockSpec((1, PEER_HEADS, tt), lambda i, j: (i, 0, 0)),
                  pl.BlockSpec((tt, d), lambda i, j: (i, 0), pipeline_mode=one),
                  gspec, vec, vec],
        out_specs=pl.BlockSpec((tt, d), lambda i, j: (i, 0)),
        out_shape=jax.ShapeDtypeStruct((rows, d), F32),
        scratch_shapes=[pltpu.VMEM((tt, d), F32), pltpu.VMEM((te, tt), F32), pltpu.VMEM((2, te, tt), BF)],
        compiler_params=_params(("parallel", "arbitrary"), 56),
        name="peer",
    )(h2, u, v, a, b, ea, eb, tau, x1, gate, g, bb)


def _pad_rows(x, rows):
    return jnp.concatenate([x, jnp.zeros((rows - x.shape[0], x.shape[1]), x.dtype)], axis=0)


def _nsa_s_body(pt_ref, *refs, npg):
    del pt_ref
    kp = refs[:npg]
    vp = refs[npg:2 * npg]
    (z_ref, ga_ref, w1_ref, pos_ref, w2_ref, bc_ref, bs_ref, ovl_ref, oc_ref, os_ref, xc_ref) = refs[2 * npg:]
    past = npg * LANES
    nb = past // CMP_STRIDE - 1
    ns = -(-(past + 4) // SEL_BLOCK)
    rows = SAMPLE_ROWS
    ga = ga_ref[0]
    tpos = past + lax.broadcasted_iota(jnp.int32, (rows, 1), 0)
    rowi = lax.broadcasted_iota(jnp.int32, (rows, LANES), 0)
    lane = lax.broadcasted_iota(jnp.int32, (rows, LANES), 1)
    cpp = LANES // CMP_STRIDE
    rstride = 2 * NSA_KV
    for g in range(NSA_KV):
        kvc = []
        for kind, pages in enumerate((kp, vp)):
            for p in range(npg):
                for s in range(CMP_STRIDE):
                    xc_ref[p * cpp:(p + 1) * cpp, s * HEAD_DIM:(s + 1) * HEAD_DIM] = (
                        pages[p][0, pl.ds(rstride * s + g, cpp, stride=rstride * CMP_STRIDE), :])
            kvc.append(_compress_core(xc_ref[...].astype(BF), w1_ref, pos_ref, w2_ref, kind))
        qs = [z_ref[0, :, QA0 + (g * NSA_REP + r) * HEAD_DIM:QA0 + (g * NSA_REP + r + 1) * HEAD_DIM].astype(BF)
              for r in range(NSA_REP)]
        outs, sel = _cmp_sel_math(qs, kvc[0], kvc[1], bc_ref[g * NSA_REP:(g + 1) * NSA_REP], tpos, nb, ns,
                                  ovl_ref[...])
        for r in range(NSA_REP):
            hd = g * NSA_REP + r
            oc_ref[0, :, hd * HEAD_DIM:(hd + 1) * HEAD_DIM] = outs[r] * ga[:, hd * 3:hd * 3 + 1]
        qst = jnp.concatenate(qs, axis=0)
        ks0 = KV0 + 2 * NSA_KV * HEAD_DIM + g * HEAD_DIM
        vs0 = KV0 + 3 * NSA_KV * HEAD_DIM + g * HEAD_DIM
        knew = _pad_rows(z_ref[0, :, ks0:ks0 + HEAD_DIM], LANES).astype(BF)
        vnew = _pad_rows(z_ref[0, :, vs0:vs0 + HEAD_DIM], LANES).astype(BF)
        bpp = LANES // SEL_BLOCK
        s_tiles, m_tiles = [], []
        for p in range(npg + 1):
            if p < npg:
                kt = kp[p][0, pl.ds(NSA_KV + g, LANES, stride=rstride), :].astype(BF)
                mk = jnp.zeros((rows, LANES), F32)
                for bq in range(bpp):
                    in_blk = (lane >= bq * SEL_BLOCK) & (lane < (bq + 1) * SEL_BLOCK)
                    mk = jnp.where(in_blk, sel[:, p * bpp + bq:p * bpp + bq + 1], mk)
                mk = mk > 0.5
            else:
                kt = knew
                mk = (lane <= rowi) & (lane < rows) & (sel[:, ns - 1:ns] > 0.5)
            bias = bs_ref[g * NSA_REP:(g + 1) * NSA_REP, :, p * LANES:(p + 1) * LANES].reshape(NSA_REP * rows, LANES)
            mk = jnp.concatenate([mk] * NSA_REP, axis=0)
            s_tiles.append(jnp.where(mk, _nt(qst, kt) * SCALE + bias, NEG_INF))
            m_tiles.append(mk)
        mx = s_tiles[0].max(axis=-1, keepdims=True)
        for st in s_tiles[1:]:
            mx = jnp.maximum(mx, st.max(axis=-1, keepdims=True))
        den = jnp.zeros((NSA_REP * rows, 1), F32)
        acc = jnp.zeros((NSA_REP * rows, HEAD_DIM), F32)
        for p in range(npg + 1):
            pr = jnp.exp(s_tiles[p] - mx) * m_tiles[p].astype(F32)
            den = den + jnp.sum(pr, axis=-1, keepdims=True)
            if p < npg:
                vt = vp[p][0, pl.ds(NSA_KV + g, LANES, stride=rstride), :].astype(BF)
            else:
                vt = vnew
            acc = acc + _dot(pr.astype(BF), vt)
        o = acc / jnp.maximum(den, 1e-30)
        for r in range(NSA_REP):
            hd = g * NSA_REP + r
            os_ref[0, :, hd * HEAD_DIM:(hd + 1) * HEAD_DIM] = o[r * rows:(r + 1) * rows] * ga[:, hd * 3 + 1:hd * 3 + 2]


def _nsa_s(pt, ck, cv, zs, ga, w1, pos, w2, bias_c, bias_s, ovl):
    bsz, npg = pt.shape
    rows = SAMPLE_ROWS
    prow = ck.shape[1]
    page = lambda p: pl.BlockSpec((1, prow, HEAD_DIM), functools.partial(lambda b, t, p: (t[b, p], 0, 0), p=p))
    const = lambda shp: pl.BlockSpec(shp, lambda b, t: (0,) * len(shp))
    in_specs = ([page(p) for p in range(npg)] + [page(p) for p in range(npg)]
                + [pl.BlockSpec((1, rows, PROJ_COLS), lambda b, t: (b, 0, 0)),
                   pl.BlockSpec((1, rows, LANES), lambda b, t: (b, 0, 0)),
                   const(w1.shape), const(pos.shape), const(w2.shape), const(bias_c.shape), const(bias_s.shape),
                   const(ovl.shape)])
    ospec = pl.BlockSpec((1, rows, NSA_WIDTH), lambda b, t: (b, 0, 0))
    gs = pltpu.PrefetchScalarGridSpec(
        num_scalar_prefetch=1, grid=(bsz,), in_specs=in_specs, out_specs=[ospec, ospec],
        scratch_shapes=[pltpu.VMEM((npg * LANES // CMP_STRIDE, CMP_STRIDE * HEAD_DIM), F32)])
    sds = jax.ShapeDtypeStruct((bsz, rows, NSA_WIDTH), F32)
    return pl.pallas_call(
        functools.partial(_nsa_s_body, npg=npg),
        grid_spec=gs, out_shape=[sds, sds],
        compiler_params=_params(("parallel",), 56),
        name="nsa_s",
    )(pt, *([ck] * npg), *([cv] * npg), zs, ga, w1, pos, w2, bias_c, bias_s, ovl)


def _win_s_body(sk_ref, sv_ref, z_ref, ga_ref, bw_ref, o_ref, nk_ref, nv_ref, *, wbuf, nnew):
    rows = SAMPLE_ROWS
    ga = ga_ref[0]
    rowi = lax.broadcasted_iota(jnp.int32, (rows, LANES), 0)
    lane = lax.broadcasted_iota(jnp.int32, (rows, LANES), 1)
    rowk = lax.broadcasted_iota(jnp.int32, (rows, wbuf), 0)
    colk = lax.broadcasted_iota(jnp.int32, (rows, wbuf), 1)
    dist_old = wbuf + rowk - colk
    m_old = jnp.concatenate([(dist_old >= 0) & (dist_old <= WINDOW)] * NSA_REP, axis=0)
    m_new = jnp.concatenate([(lane <= rowi) & (lane < rows) & (rowi - lane <= WINDOW)] * NSA_REP, axis=0)
    for g in range(NSA_KV):
        qst = jnp.concatenate(
            [z_ref[0, :, QA0 + (g * NSA_REP + r) * HEAD_DIM:QA0 + (g * NSA_REP + r + 1) * HEAD_DIM]
             for r in range(NSA_REP)], axis=0).astype(BF)
        kw0 = KV0 + 4 * NSA_KV * HEAD_DIM + g * HEAD_DIM
        vw0 = KV0 + 5 * NSA_KV * HEAD_DIM + g * HEAD_DIM
        kold = sk_ref[0, pl.ds(g, wbuf, stride=NSA_KV), :].astype(BF)
        vold = sv_ref[0, pl.ds(g, wbuf, stride=NSA_KV), :].astype(BF)
        knew = _pad_rows(z_ref[0, :, kw0:kw0 + HEAD_DIM], LANES).astype(BF)
        vnew = _pad_rows(z_ref[0, :, vw0:vw0 + HEAD_DIM], LANES).astype(BF)
        b_old = bw_ref[g * NSA_REP:(g + 1) * NSA_REP, :, 0:wbuf].reshape(NSA_REP * rows, wbuf)
        b_new = bw_ref[g * NSA_REP:(g + 1) * NSA_REP, :, wbuf:wbuf + LANES].reshape(NSA_REP * rows, LANES)
        s_old = jnp.where(m_old, _nt(qst, kold) * SCALE + b_old, NEG_INF)
        s_new = jnp.where(m_new, _nt(qst, knew) * SCALE + b_new, NEG_INF)
        mx = jnp.maximum(s_old.max(axis=-1, keepdims=True), s_new.max(axis=-1, keepdims=True))
        p_old = jnp.exp(s_old - mx) * m_old.astype(F32)
        p_new = jnp.exp(s_new - mx) * m_new.astype(F32)
        den = jnp.sum(p_old, axis=-1, keepdims=True) + jnp.sum(p_new, axis=-1, keepdims=True)
        o = (_dot(p_old.astype(BF), vold) + _dot(p_new.astype(BF), vnew)) / jnp.maximum(den, 1e-30)
        for r in range(NSA_REP):
            hd = g * NSA_REP + r
            o_ref[0, :, hd * HEAD_DIM:(hd + 1) * HEAD_DIM] = o[r * rows:(r + 1) * rows] * ga[:, hd * 3 + 2:hd * 3 + 3]
    keep = (wbuf - nnew) * NSA_KV
    nk_ref[0, 0:keep, :] = sk_ref[0, nnew * NSA_KV:wbuf * NSA_KV, :]
    nv_ref[0, 0:keep, :] = sv_ref[0, nnew * NSA_KV:wbuf * NSA_KV, :]
    for (c0, dst) in ((KV0 + 4 * NSA_KV * HEAD_DIM, nk_ref), (KV0 + 5 * NSA_KV * HEAD_DIM, nv_ref)):
        new = jnp.concatenate(
            [z_ref[0, i:i + 1, c0 + g * HEAD_DIM:c0 + (g + 1) * HEAD_DIM] for i in range(nnew) for g in range(NSA_KV)],
            axis=0)
        dst[0, keep:wbuf * NSA_KV, :] = new


def _win_s(st_k, st_v, layer, zs, ga, bias_w, nnew):
    bsz = zs.shape[0]
    rows = SAMPLE_ROWS
    srows = st_k.shape[1]
    wbuf = srows // NSA_KV
    sspec = pl.BlockSpec((1, srows, HEAD_DIM), lambda b: (layer * bsz + b, 0, 0))
    nspec = pl.BlockSpec((1, srows, HEAD_DIM), lambda b: (b, 0, 0))
    nsds = jax.ShapeDtypeStruct((bsz, srows, HEAD_DIM), F32)
    return pl.pallas_call(
        functools.partial(_win_s_body, wbuf=wbuf, nnew=nnew),
        grid=(bsz,),
        in_specs=[sspec, sspec,
                  pl.BlockSpec((1, rows, PROJ_COLS), lambda b: (b, 0, 0)),
                  pl.BlockSpec((1, rows, LANES), lambda b: (b, 0, 0)),
                  pl.BlockSpec(bias_w.shape, lambda b: (0, 0, 0))],
        out_specs=[pl.BlockSpec((1, rows, NSA_WIDTH), lambda b: (b, 0, 0)), nspec, nspec],
        out_shape=[jax.ShapeDtypeStruct((bsz, rows, NSA_WIDTH), F32), nsds, nsds],
        compiler_params=_params(("parallel",), 32),
        name="win_s",
    )(st_k, st_v, zs, ga, bias_w)


def _fox_s_body(pt_ref, *refs, npg):
    del pt_ref
    kp = refs[:npg]
    vp = refs[npg:2 * npg]
    fp = refs[2 * npg:3 * npg]
    z_ref, lf_ref, o_ref = refs[3 * npg:]
    rows = SAMPLE_ROWS
    rowi = lax.broadcasted_iota(jnp.int32, (rows, LANES), 0)
    lane = lax.broadcasted_iota(jnp.int32, (rows, LANES), 1)
    r128 = lax.broadcasted_iota(jnp.int32, (LANES, LANES), 0)
    c128 = lax.broadcasted_iota(jnp.int32, (LANES, LANES), 1)
    upper = (r128 <= c128).astype(BF)
    ft = []
    carry = jnp.zeros((LANES, 1), F32)
    for p in range(npg):
        lp = fp[p][0]
        lpad = jnp.concatenate([lp, jnp.zeros((LANES, LANES - FOX_HEADS), F32)], axis=1)
        hi, mid, lo = _split3(lpad)
        cs = _tn(hi, upper) + _tn(mid, upper) + _tn(lo, upper) + carry
        carry = cs[:, LANES - 1:LANES]
        ft.append(cs[0:FOX_HEADS])
    lf_new = lf_ref[0]
    x = lf_new
    for sh in (1, 2, 4):
        x = x + jnp.where(rowi >= sh, pltpu.roll(x, sh, 0), 0.0)
    tot_t = jnp.broadcast_to(carry, (LANES, LANES)).T
    tot_row = pltpu.roll(tot_t[0:1, :], FB_LANE, 1)
    f_new = x + tot_row
    f_new_t = _pad_rows(f_new, LANES).T
    m_new = (lane <= rowi) & (lane < rows)
    for h in range(FOX_HEADS):
        q = z_ref[0, :, QB0 + h * HEAD_DIM:QB0 + (h + 1) * HEAD_DIM].astype(BF)
        k0 = QB0 + FOX_WIDTH + h * HEAD_DIM
        v0 = QB0 + 2 * FOX_WIDTH + h * HEAD_DIM
        knew = _pad_rows(z_ref[0, :, k0:k0 + HEAD_DIM], LANES).astype(BF)
        vnew = _pad_rows(z_ref[0, :, v0:v0 + HEAD_DIM], LANES).astype(BF)
        fq = f_new[:, FB_LANE + h:FB_LANE + h + 1]
        s_tiles = []
        for p in range(npg):
            kt = kp[p][0, pl.ds(h, LANES, stride=FOX_HEADS), :].astype(BF)
            s_tiles.append(_nt(q, kt) * SCALE + (fq - ft[p][h:h + 1, :]))
        s_n = _nt(q, knew) * SCALE + (fq - f_new_t[FB_LANE + h:FB_LANE + h + 1, :])
        s_tiles.append(jnp.where(m_new, s_n, NEG_INF))
        mx = s_tiles[0].max(axis=-1, keepdims=True)
        for st in s_tiles[1:]:
            mx = jnp.maximum(mx, st.max(axis=-1, keepdims=True))
        den = jnp.zeros((rows, 1), F32)
        acc = jnp.zeros((rows, HEAD_DIM), F32)
        for p in range(npg + 1):
            pr = jnp.exp(s_tiles[p] - mx)
            if p == npg:
                pr = pr * m_new.astype(F32)
                vt = vnew
            else:
                vt = vp[p][0, pl.ds(h, LANES, stride=FOX_HEADS), :].astype(BF)
            den = den + jnp.sum(pr, axis=-1, keepdims=True)
            acc = acc + _dot(pr.astype(BF), vt)
        o_ref[0, :, h * HEAD_DIM:(h + 1) * HEAD_DIM] = acc / jnp.maximum(den, 1e-30)


def _fox_s(pt, ck, cv, cf, zs, lf):
    bsz, npg = pt.shape
    rows = SAMPLE_ROWS
    prow = ck.shape[1]
    pmap = lambda p: functools.partial(lambda b, t, p: (t[b, p], 0, 0), p=p)
    in_specs = ([pl.BlockSpec((1, prow, HEAD_DIM), pmap(p)) for p in range(npg)]
                + [pl.BlockSpec((1, prow, HEAD_DIM), pmap(p)) for p in range(npg)]
                + [pl.BlockSpec((1, LANES, FOX_HEADS), pmap(p)) for p in range(npg)]
                + [pl.BlockSpec((1, rows, PROJ_COLS), lambda b, t: (b, 0, 0)),
                   pl.BlockSpec((1, rows, LANES), lambda b, t: (b, 0, 0))])
    gs = pltpu.PrefetchScalarGridSpec(
        num_scalar_prefetch=1, grid=(bsz,), in_specs=in_specs,
        out_specs=pl.BlockSpec((1, rows, FOX_WIDTH), lambda b, t: (b, 0, 0)))
    return pl.pallas_call(
        functools.partial(_fox_s_body, npg=npg),
        grid_spec=gs, out_shape=jax.ShapeDtypeStruct((bsz, rows, FOX_WIDTH), F32),
        compiler_params=_params(("parallel",), 56),
        name="fox_s",
    )(pt, *([ck] * npg), *([cv] * npg), *([cf] * npg), zs, lf)


def _rel_bucket(dist):
    n = jnp.maximum(dist, 0)
    exact = REL_BUCKETS // 2
    scaled = jnp.log(jnp.maximum(n, 1).astype(F32) / exact) / math.log(REL_MAX_DIST / exact)
    large = jnp.minimum(exact + (scaled * (REL_BUCKETS - exact)).astype(jnp.int32), REL_BUCKETS - 1)
    return jnp.where(n < exact, n, large)


def _bias_of(rel_table, dist):
    bucket = _rel_bucket(dist)
    out = jnp.zeros((rel_table.shape[1],) + dist.shape, F32)
    for b in range(REL_BUCKETS):
        row = rel_table[b].reshape((-1,) + (1,) * dist.ndim)
        out = jnp.where(bucket[None] == b, row, out)
    return out


def _overlap(nb, ns):
    i = np.arange(LANES)[:, None]
    j = np.arange(LANES)[None, :]
    rs = SEL_BLOCK // CMP_STRIDE
    rc = CMP_LEN // CMP_STRIDE
    m = np.zeros((LANES, LANES), np.float32)
    for a in range(rs):
        for b in range(rc):
            m = m + (i == j * rs + a + b - (rc - 1))
    m = m * (i < nb) * (j < ns)
    return jnp.asarray(m, BF)


def kernel(x_prompt, x_sample, cache_nsa_k, cache_nsa_v, cache_fox_k, cache_fox_v, cache_fox_logf,
           state_nsa_win_k, state_nsa_win_v, page_table, c_prompt, c_sample, w_ada, b_ada, w_in,
           b_forget, cmp_pos, cmp_w1, cmp_w2, rel_table, w_branch_a, w_branch_b, w_out, ln_g, ln_b,
           peer_wq, peer_subkeys, peer_u, peer_v):
    bp, seq, d = x_prompt.shape
    bs, dec = x_sample.shape[:2]
    depth, n_pool, page = cache_nsa_k.shape[:3]
    npg = page_table.shape[1]
    past = npg * page
    wbuf = state_nsa_win_k.shape[2]
    rows = SAMPLE_ROWS
    assert d == D_MODEL and page == LANES and depth == DEPTH and dec <= rows
    assert seq % FLASH_T == 0 or seq < FLASH_T

    ck_nsa = cache_nsa_k.reshape(depth * n_pool, page * 2 * NSA_KV, HEAD_DIM)
    cv_nsa = cache_nsa_v.reshape(depth * n_pool, page * 2 * NSA_KV, HEAD_DIM)
    ck_fox = cache_fox_k.reshape(depth * n_pool, page * FOX_HEADS, HEAD_DIM)
    cv_fox = cache_fox_v.reshape(depth * n_pool, page * FOX_HEADS, HEAD_DIM)
    cf_fox = cache_fox_logf.reshape(depth * n_pool, page, FOX_HEADS)
    st_k = state_nsa_win_k.reshape(depth * bs, wbuf * NSA_KV, HEAD_DIM)
    st_v = state_nsa_win_v.reshape(depth * bs, wbuf * NSA_KV, HEAD_DIM)

    tq = _flash_tile("sel", seq)
    ii = jnp.arange(tq)

    def tile_bias(dd):
        b = _bias_of(rel_table, dd * tq + ii[None, :] - ii[:, None])
        return b.reshape(NSA_KV, NSA_REP, tq, tq).transpose(0, 2, 1, 3).reshape(NSA_KV, tq, NSA_REP * tq)

    bias_tile = jnp.stack([tile_bias(dd) for dd in range(3)])
    nb_p = seq // CMP_STRIDE - 1
    ns_p = -(-seq // SEL_BLOCK)
    lane = jnp.arange(LANES)
    bias_cmp_p = _bias_of(rel_table, jnp.arange(seq)[:, None] - (lane * CMP_STRIDE + CMP_LEN - 1)[None, :])
    nb_s = past // CMP_STRIDE - 1
    ns_s = -(-(past + dec) // SEL_BLOCK)
    tq_s = past + jnp.arange(rows)
    bias_cmp_s = _bias_of(rel_table, tq_s[:, None] - (lane * CMP_STRIDE + CMP_LEN - 1)[None, :])
    bias_sel_s = _bias_of(rel_table, tq_s[:, None] - jnp.arange(past + LANES)[None, :])
    bias_win_s = _bias_of(rel_table, wbuf + jnp.arange(rows)[:, None] - jnp.arange(wbuf + LANES)[None, :])
    ovl_p = _overlap(nb_p, ns_p).T
    ovl_s = _overlap(nb_s, ns_s)

    c_all = jnp.concatenate([c_prompt, c_sample], axis=0)
    xp = x_prompt
    xs = x_sample
    outs_p = {k: [] for k in ("nsa_k", "nsa_v", "fox_k", "fox_v", "logf", "win_k", "win_v")}
    outs_s = {k: [] for k in ("nsa_k", "nsa_v", "fox_k", "fox_v", "logf", "win_k", "win_v")}
    gh = NSA_KV * HEAD_DIM

    for l in range(depth):
        wl = w_in[l]
        o_ga = NSA_WIDTH + 6 * gh
        o_qb = o_ga + 3 * NSA_HEADS
        o_fb = o_qb + 3 * FOX_WIDTH
        o_gm = o_fb + FOX_HEADS
        wcat = jnp.concatenate(
            [wl[:, :o_ga], wl[:, o_qb:o_fb], wl[:, o_gm:], wl[:, o_ga:o_qb], wl[:, o_fb:o_gm],
             jnp.zeros((d, PROJ_COLS - GF0 - 3 * NSA_HEADS - FOX_HEADS), F32)], axis=1).astype(BF)
        bfpad = jnp.zeros((1, LANES), F32).at[0, FB_LANE:FB_LANE + FOX_HEADS].set(b_forget[l])
        w1 = cmp_w1[l].astype(BF)
        pos = cmp_pos[l].reshape(2, 1, CMP_LEN * HEAD_DIM)
        w2 = cmp_w2[l].astype(BF)
        wa = w_branch_a[l].astype(BF)
        wb = w_branch_b[l].astype(BF)
        wo = w_out[l].astype(BF)
        wq = peer_wq[l].astype(BF)
        sk = peer_subkeys[l].astype(BF)
        pu = peer_u[l].astype(BF)
        pv = peer_v[l].astype(BF)
        g1, b1 = ln_g[l, 0][None], ln_b[l, 0][None]
        g2, b2 = ln_g[l, 1][None], ln_b[l, 1][None]
        pt_l = page_table + l * n_pool

        mod = _ada(c_all, w_ada[l], b_ada[l][None]).reshape(bp + bs, 6, d)
        mod_p = mod[:bp]
        mod_s = mod[bp:]

        z = _proj(xp, mod_p[:, 0:1], mod_p[:, 1:2], wcat, tm=min(1024, seq), tn=1024)
        ga, lf, fcum = _gates(z, bfpad)
        kvc = _compress_p(z, w1, pos, w2)
        o_cmp, sel = _cmp_p(z, kvc, bias_cmp_p, ga, ovl_p, nb_p, ns_p)
        o_sel = _flash("sel", z, (bias_tile, ga, sel))
        o_win = _flash("win", z, (bias_tile, ga))
        f_t = jnp.swapaxes(fcum[:, :, FB_LANE:FB_LANE + FOX_HEADS], 1, 2)
        o_b = _flash("fox", z, (fcum, f_t))
        nrow = bp * seq
        u = _merge1(o_cmp.reshape(nrow, -1), o_sel.reshape(nrow, -1), o_win.reshape(nrow, -1),
                    o_b.reshape(nrow, -1), z.reshape(nrow, -1), wa, wb)
        x1 = _merge2(u.reshape(bp, seq, d), wo, xp, mod_p[:, 2:3], g1, b1)
        q, h2 = _proj(x1, mod_p[:, 3:4], mod_p[:, 4:5], wq, tm=min(1024, seq), tn=1024, emit_h=True)
        ra, rb, rea, reb, tau = _route(q.reshape(nrow, -1), sk)
        xp = _peer(h2.reshape(nrow, d), pu, pv, ra, rb, rea, reb, tau, x1.reshape(nrow, d),
                   mod_p[:, 5:6], g2, b2, rows_per_gate=seq).reshape(bp, seq, d)
        keep = min(WINDOW, seq)

        def piece(i0, r0=0):
            return z[:, r0:, KV0 + i0 * gh:KV0 + (i0 + 1) * gh].reshape(bp, seq - r0, NSA_KV, HEAD_DIM)

        outs_p["nsa_k"].append(jnp.stack([piece(0), piece(2)], axis=2))
        outs_p["nsa_v"].append(jnp.stack([piece(1), piece(3)], axis=2))
        outs_p["fox_k"].append(z[:, :, QB0 + FOX_WIDTH:QB0 + 2 * FOX_WIDTH].reshape(bp, seq, FOX_HEADS, HEAD_DIM))
        outs_p["fox_v"].append(z[:, :, QB0 + 2 * FOX_WIDTH:QB0 + 3 * FOX_WIDTH].reshape(bp, seq, FOX_HEADS, HEAD_DIM))
        outs_p["logf"].append(lf[:, :, FB_LANE:FB_LANE + FOX_HEADS])
        outs_p["win_k"].append(piece(4, seq - keep))
        outs_p["win_v"].append(piece(5, seq - keep))

        xs8 = jnp.pad(xs, ((0, 0), (0, rows - dec), (0, 0)))
        mrow = lambda k: jnp.repeat(mod_s[:, k], rows, axis=0)[None]
        zs = _proj(xs8.reshape(1, bs * rows, d), mrow(0), mrow(1), wcat, tm=min(512, bs * rows), tn=1024)
        ga_s, lf_s, _ = _gates(zs, bfpad)
        zs = zs.reshape(bs, rows, PROJ_COLS)
        ga_s = ga_s.reshape(bs, rows, LANES)
        lf_s = lf_s.reshape(bs, rows, LANES)
        oc_s, os_s = _nsa_s(pt_l, ck_nsa, cv_nsa, zs, ga_s, w1, pos, w2, bias_cmp_s, bias_sel_s, ovl_s)
        ow_s, nk_s, nv_s = _win_s(st_k, st_v, l, zs, ga_s, bias_win_s, dec)
        ob_s = _fox_s(pt_l, ck_fox, cv_fox, cf_fox, zs, lf_s)
        srow = bs * rows
        u_s = _merge1(oc_s.reshape(srow, -1), os_s.reshape(srow, -1), ow_s.reshape(srow, -1),
                      ob_s.reshape(srow, -1), zs.reshape(srow, -1), wa, wb)
        x1_s = _merge2(u_s.reshape(1, srow, d), wo, xs8.reshape(1, srow, d), mrow(2), g1, b1)
        x1_c = x1_s.reshape(bs, rows, d)[:, :dec].reshape(1, bs * dec, d)
        crow = lambda k: jnp.repeat(mod_s[:, k], dec, axis=0)[None]
        q_s, h2_s = _proj(x1_c, crow(3), crow(4), wq, tm=min(512, bs * dec), tn=1024, emit_h=True)
        ra, rb, rea, reb, tau = _route(q_s.reshape(bs * dec, -1), sk)
        xs = _peer(h2_s.reshape(bs * dec, d), pu, pv, ra, rb, rea, reb, tau, x1_c.reshape(bs * dec, d),
                   crow(5).reshape(-1, PEER_TOK, d), g2, b2, rows_per_gate=1).reshape(bs, dec, d)
        kv_s = zs[:, :dec, KV0:KV0 + 6 * gh].reshape(bs, dec, 6, NSA_KV, HEAD_DIM)
        outs_s["nsa_k"].append(kv_s[:, :, 0:4:2])
        outs_s["nsa_v"].append(kv_s[:, :, 1:4:2])
        outs_s["fox_k"].append(zs[:, :dec, QB0 + FOX_WIDTH:QB0 + 2 * FOX_WIDTH].reshape(bs, dec, FOX_HEADS, HEAD_DIM))
        outs_s["fox_v"].append(zs[:, :dec, QB0 + 2 * FOX_WIDTH:QB0 + 3 * FOX_WIDTH].reshape(bs, dec, FOX_HEADS, HEAD_DIM))
        outs_s["logf"].append(lf_s[:, :dec, FB_LANE:FB_LANE + FOX_HEADS])
        outs_s["win_k"].append(nk_s.reshape(bs, wbuf, NSA_KV, HEAD_DIM))
        outs_s["win_v"].append(nv_s.reshape(bs, wbuf, NSA_KV, HEAD_DIM))

    st = lambda dct, k: jnp.stack(dct[k])
    return (xp, xs,
            st(outs_p, "nsa_k"), st(outs_p, "nsa_v"), st(outs_p, "fox_k"), st(outs_p, "fox_v"),
            st(outs_p, "logf"), st(outs_p, "win_k"), st(outs_p, "win_v"),
            st(outs_s, "nsa_k"), st(outs_s, "nsa_v"), st(outs_s, "fox_k"), st(outs_s, "fox_v"),
            st(outs_s, "logf"), st(outs_s, "win_k"), st(outs_s, "win_v"))
```

```python
import functools
import math

import numpy as np
import jax
import jax.numpy as jnp
from jax import lax
from jax.experimental import pallas as pl
from jax.experimental.pallas import tpu as pltpu

F32 = jnp.float32
BF = jnp.bfloat16

D_MODEL = 2048
DEPTH = 2
HEAD_DIM = 128
NSA_HEADS = D_MODEL // (2 * HEAD_DIM)
NSA_KV = max(1, NSA_HEADS // 4)
NSA_REP = NSA_HEADS // NSA_KV
FOX_HEADS = D_MODEL // (2 * HEAD_DIM)
NSA_WIDTH = NSA_HEADS * HEAD_DIM
FOX_WIDTH = FOX_HEADS * HEAD_DIM
CMP_LEN = 32
CMP_STRIDE = 16
CMP_HID = 256
SEL_BLOCK = 64
SEL_TOPN = 16
WINDOW = 512
REL_BUCKETS = 32
REL_MAX_DIST = 128
PEER_HEADS = 8
PEER_NKEYS = 128
PEER_EXPERTS = PEER_NKEYS * PEER_NKEYS
PEER_DK = 256
PEER_TOPK = 16
DEEPNORM_ALPHA = (2 * DEPTH) ** 0.25
LN_EPS = 1e-5
NEG_INF = -1e9
FORCE_SCORE = 1e9
SCALE = HEAD_DIM ** -0.5

LANES = 128
SUB = 8
VMEM_MB = 1024 * 1024

QA0 = 0
KV0 = QA0 + NSA_WIDTH
QB0 = KV0 + 6 * NSA_KV * HEAD_DIM
GM0 = QB0 + 3 * FOX_WIDTH
GF0 = GM0 + 2 * D_MODEL
PROJ_COLS = 10240
FB_LANE = 3 * NSA_HEADS
SAMPLE_ROWS = 8
PEER_TOK = 512
PEER_ETILE = 512
FLASH_T = 256
FOX_T = 512
FLASH_HPS = 2


def _nt(a, b):
    return lax.dot_general(a, b, (((1,), (1,)), ((), ())), preferred_element_type=F32)


def _tn(a, b):
    return lax.dot_general(a, b, (((0,), (0,)), ((), ())), preferred_element_type=F32)


def _dot(a, b):
    return jnp.dot(a, b, preferred_element_type=F32)


def _split3(x):
    hi = x.astype(BF)
    r1 = x - hi.astype(F32)
    mid = r1.astype(BF)
    lo = (r1 - mid.astype(F32)).astype(BF)
    return hi, mid, lo


def _params(sem, mb):
    return pltpu.CompilerParams(dimension_semantics=sem, vmem_limit_bytes=mb * VMEM_MB)


def _lane_col(x, idx):
    lane = lax.broadcasted_iota(jnp.int32, x.shape, 1)
    return jnp.sum(jnp.where(lane == idx, x, 0.0), axis=-1, keepdims=True)


def _ada_body(c_ref, w_ref, b_ref, o_ref):
    c = c_ref[...]
    a = (c * jax.nn.sigmoid(c)).astype(BF)
    o_ref[...] = _dot(a, w_ref[...].astype(BF)) + b_ref[...]


def _ada(c, w, b):
    m, d = c.shape
    n = w.shape[1]
    tn = 1536
    return pl.pallas_call(
        _ada_body,
        grid=(n // tn,),
        in_specs=[pl.BlockSpec((m, d), lambda j: (0, 0)),
                  pl.BlockSpec((d, tn), lambda j: (0, j)),
                  pl.BlockSpec((1, tn), lambda j: (0, j))],
        out_specs=pl.BlockSpec((m, tn), lambda j: (0, j)),
        out_shape=jax.ShapeDtypeStruct((m, n), F32),
        compiler_params=_params(("parallel",), 48),
        name="ada",
    )(c, w, b)


def _proj_body(x_ref, sh_ref, sc_ref, w_ref, o_ref, *rest, emit_h):
    if emit_h:
        hb_ref, h_ref = rest
    else:
        (h_ref,) = rest

    @pl.when(pl.program_id(2) == 0)
    def _():
        h = (x_ref[0] * (1.0 + sc_ref[0]) + sh_ref[0]).astype(BF)
        h_ref[...] = h
        if emit_h:
            hb_ref[0] = h

    o_ref[0] = _dot(h_ref[...], w_ref[...])


def _proj(x, shift, scale, w, tm, tn, emit_h=False):
    n, t, d = x.shape
    c = w.shape[1]
    ts = shift.shape[1]
    tms = tm if ts == t else 1
    mod_map = (lambda b, i, j: (b, i, 0)) if ts == t else (lambda b, i, j: (b, 0, 0))
    out_shape = [jax.ShapeDtypeStruct((n, t, c), F32)]
    out_specs = [pl.BlockSpec((1, tm, tn), lambda b, i, j: (b, i, j))]
    if emit_h:
        out_shape.append(jax.ShapeDtypeStruct((n, t, d), BF))
        out_specs.append(pl.BlockSpec((1, tm, d), lambda b, i, j: (b, i, 0)))
    res = pl.pallas_call(
        functools.partial(_proj_body, emit_h=emit_h),
        grid=(n, t // tm, c // tn),
        in_specs=[pl.BlockSpec((1, tm, d), lambda b, i, j: (b, i, 0)),
                  pl.BlockSpec((1, tms, d), mod_map),
                  pl.BlockSpec((1, tms, d), mod_map),
                  pl.BlockSpec((d, tn), lambda b, i, j: (0, j))],
        out_specs=out_specs,
        out_shape=out_shape,
        scratch_shapes=[pltpu.VMEM((tm, d), BF)],
        compiler_params=_params(("parallel", "parallel", "arbitrary"), 56),
        name="proj",
    )(x, shift, scale, w)
    return res if emit_h else res[0]


def _cumsum_rows(x, tri):
    hi, mid, lo = _split3(x)
    return _dot(tri, hi) + _dot(tri, mid) + _dot(tri, lo)


def _gates_body(z_ref, bf_ref, ga_ref, lf_ref, f_ref, *, t, blk):
    z = z_ref[0]
    ga_ref[0] = jax.nn.sigmoid(z)
    lf = jax.nn.log_sigmoid(z + bf_ref[...])
    lf_ref[0] = lf
    r = lax.broadcasted_iota(jnp.int32, (blk, blk), 0)
    c = lax.broadcasted_iota(jnp.int32, (blk, blk), 1)
    tri = (r >= c).astype(BF)
    carry = jnp.zeros((1, LANES), F32)
    for b in range(t // blk):
        fb = _cumsum_rows(lf[b * blk:(b + 1) * blk], tri) + carry
        f_ref[0, b * blk:(b + 1) * blk, :] = fb
        carry = fb[blk - 1:blk, :]


def _gates(z, bfpad):
    n, t, _ = z.shape
    blk = min(256, t)
    sds = jax.ShapeDtypeStruct((n, t, LANES), F32)
    spec = pl.BlockSpec((1, t, LANES), lambda b: (b, 0, 0))
    return pl.pallas_call(
        functools.partial(_gates_body, t=t, blk=blk),
        grid=(n,),
        in_specs=[pl.BlockSpec((1, t, LANES), lambda b: (b, 0, GF0 // LANES)),
                  pl.BlockSpec((1, LANES), lambda b: (0, 0))],
        out_specs=[spec, spec, spec],
        out_shape=[sds, sds, sds],
        compiler_params=_params(("parallel",), 32),
        name="gates",
    )(z, bfpad)


def _compress_core(xc, w1_ref, pos_ref, w2_ref, kind):
    m = xc.shape[0]
    half = CMP_STRIDE * HEAD_DIM
    first = _dot(xc, w1_ref[kind, 0:half, :])
    second = _dot(xc, w1_ref[kind, half:2 * half, :])
    posb = jnp.broadcast_to(pos_ref[kind], (SUB, 2 * half)).astype(BF)
    pos_term = _dot(posb, w1_ref[kind])[0:1]
    hid = jax.nn.gelu(first + pltpu.roll(second, m - 1, 0) + pos_term)
    return _dot(hid.astype(BF), w2_ref[kind])


def _compress_p_body(x_ref, w1_ref, pos_ref, w2_ref, o_ref, xc_ref, *, m):
    kind = pl.program_id(1) // NSA_KV
    for s in range(CMP_STRIDE):
        xc_ref[:, s * HEAD_DIM:(s + 1) * HEAD_DIM] = x_ref[0, pl.ds(s, m, stride=CMP_STRIDE), :].astype(BF)
    o_ref[0, 0] = _compress_core(xc_ref[...], w1_ref, pos_ref, w2_ref, kind)


def _compress_p(z, w1, pos, w2):
    n, t, _ = z.shape
    m = t // CMP_STRIDE
    return pl.pallas_call(
        functools.partial(_compress_p_body, m=m),
        grid=(n, 2 * NSA_KV),
        in_specs=[pl.BlockSpec((1, t, HEAD_DIM), lambda b, a: (b, 0, KV0 // HEAD_DIM + a)),
                  pl.BlockSpec(w1.shape, lambda b, a: (0, 0, 0)),
                  pl.BlockSpec(pos.shape, lambda b, a: (0, 0, 0)),
                  pl.BlockSpec(w2.shape, lambda b, a: (0, 0, 0))],
        out_specs=pl.BlockSpec((1, 1, m, HEAD_DIM), lambda b, a: (b, a, 0, 0)),
        out_shape=jax.ShapeDtypeStruct((n, 2 * NSA_KV, m, HEAD_DIM), F32),
        scratch_shapes=[pltpu.VMEM((m, CMP_STRIDE * HEAD_DIM), BF)],
        compiler_params=_params(("parallel", "parallel"), 32),
        name="compress_p",
    )(z, w1, pos, w2)


def _masked_softmax(s, mask):
    s = jnp.where(mask, s, NEG_INF)
    p = jnp.exp(s - jnp.max(s, axis=-1, keepdims=True)) * mask.astype(F32)
    return p / jnp.maximum(jnp.sum(p, axis=-1, keepdims=True), 1e-30)


def _cmp_sel_math(qs, kc, vc, bias, tpos, nb, ns, ovl, blocks_on_rows=False, tpos_row=None):
    tq = qs[0].shape[0]
    lane = lax.broadcasted_iota(jnp.int32, (tq, LANES), 1)
    dist = tpos - (lane * CMP_STRIDE + (CMP_LEN - 1))
    mask = (dist >= 0) & (lane < nb)
    kcb = kc.astype(BF)
    vcb = vc.astype(BF)
    outs = []
    psum = jnp.zeros((tq, LANES), F32)
    for r in range(NSA_REP):
        p = _masked_softmax(_nt(qs[r], kcb) * SCALE + bias[r], mask)
        outs.append(_dot(p.astype(BF), vcb))
        psum = psum + p
    hi, mid, lo = _split3(psum)
    if blocks_on_rows:
        nr = -(-ns // SUB) * SUB
        imp = (_nt(ovl, hi) + _nt(ovl, mid) + _nt(ovl, lo))[0:nr]
        blk = lax.broadcasted_iota(jnp.int32, (nr, tq), 0)
        trow = tpos_row
        cur = trow // SEL_BLOCK
        forced = (blk == 0) | (blk == cur) | (blk == cur - 1)
        score = jnp.where(forced, FORCE_SCORE, jnp.where(blk * SEL_BLOCK <= trow, imp, NEG_INF))
        score = jnp.where(blk < ns, score, -3.0e38)
        rank = jnp.zeros((nr, tq), F32)
        for j in range(ns):
            rj = score[j:j + 1, :]
            rank = rank + ((rj > score) | ((rj == score) & (blk > j))).astype(F32)
        sel = ((rank < float(min(SEL_TOPN, ns))) & (blk < ns)).astype(F32)
        return outs, jnp.concatenate([sel, jnp.zeros((LANES - nr, tq), F32)], axis=0)
    imp = _dot(hi, ovl) + _dot(mid, ovl) + _dot(lo, ovl)
    cur = tpos // SEL_BLOCK
    valid = lane * SEL_BLOCK <= tpos
    forced = (lane == 0) | (lane == cur) | (lane == cur - 1)
    score = jnp.where(forced, FORCE_SCORE, jnp.where(valid, imp, NEG_INF))
    score = jnp.where(lane < ns, score, -3.0e38)
    rank = jnp.zeros((tq, LANES), F32)
    for j in range(ns):
        cj = score[:, j:j + 1]
        ahead = (cj > score) | ((cj == score) & (lane > j))
        rank = rank + ahead.astype(F32)
    sel = (rank < float(min(SEL_TOPN, ns))) & (lane < ns)
    return outs, sel.astype(F32)


def _cmp_p_body(q_ref, kc_ref, vc_ref, bias_ref, ga_ref, ovl_ref, o_ref, sel_ref, *, tq, nb, ns):
    g = pl.program_id(1)
    qi = pl.program_id(2)
    q = q_ref[0]
    qs = [q[:, r * HEAD_DIM:(r + 1) * HEAD_DIM].astype(BF) for r in range(NSA_REP)]
    tpos = qi * tq + lax.broadcasted_iota(jnp.int32, (tq, 1), 0)
    tpos_row = qi * tq + lax.broadcasted_iota(jnp.int32, (1, tq), 1)
    outs, sel_t = _cmp_sel_math(qs, kc_ref[0, 0], vc_ref[0, 0], bias_ref[...], tpos, nb, ns, ovl_ref[...],
                                blocks_on_rows=True, tpos_row=tpos_row)
    ga = ga_ref[0]
    for r in range(NSA_REP):
        gate = _lane_col(ga, (g * NSA_REP + r) * 3 + 0)
        o_ref[0, :, r * HEAD_DIM:(r + 1) * HEAD_DIM] = outs[r] * gate
    sel_ref[0, 0] = sel_t


def _cmp_p(z, kvc, bias_cmp, ga, ovl, nb, ns):
    n, t, _ = z.shape
    tq = min(FLASH_T, t)
    gw = NSA_REP * HEAD_DIM
    m = kvc.shape[2]
    return pl.pallas_call(
        functools.partial(_cmp_p_body, tq=tq, nb=nb, ns=ns),
        grid=(n, NSA_KV, t // tq),
        in_specs=[pl.BlockSpec((1, tq, gw), lambda b, g, i: (b, i, g)),
                  pl.BlockSpec((1, 1, m, HEAD_DIM), lambda b, g, i: (b, g, 0, 0)),
                  pl.BlockSpec((1, 1, m, HEAD_DIM), lambda b, g, i: (b, NSA_KV + g, 0, 0)),
                  pl.BlockSpec((NSA_REP, tq, LANES), lambda b, g, i: (g, i, 0)),
                  pl.BlockSpec((1, tq, LANES), lambda b, g, i: (b, i, 0)),
                  pl.BlockSpec((LANES, LANES), lambda b, g, i: (0, 0))],
        out_specs=[pl.BlockSpec((1, tq, gw), lambda b, g, i: (b, i, g)),
                   pl.BlockSpec((1, 1, LANES, tq), lambda b, g, i: (b, g, 0, i))],
        out_shape=[jax.ShapeDtypeStruct((n, t, NSA_WIDTH), F32),
                   jax.ShapeDtypeStruct((n, NSA_KV, LANES, t), F32)],
        compiler_params=_params(("parallel", "parallel", "parallel"), 32),
        name="cmp_p",
    )(z, kvc, kvc, bias_cmp, ga, ovl)


def _flash_body(tab_ref, *refs, mode, tq, hps):
    if mode == "sel":
        q_ref, k_ref, v_ref, bias_ref, ga_ref, sel_ref, o_ref, m_ref, l_ref, acc_ref = refs
    elif mode == "win":
        q_ref, k_ref, v_ref, bias_ref, ga_ref, o_ref, m_ref, l_ref, acc_ref = refs
    else:
        q_ref, k_ref, v_ref, fcol_ref, frow_ref, o_ref, m_ref, l_ref, acc_ref = refs
    rep = 1 if mode == "fox" else NSA_REP
    w = rep * HEAD_DIM
    hg0 = pl.program_id(1) * hps
    pair = pl.program_id(2)
    qi = tab_ref[0, pair]
    tile = tab_ref[1, pair]

    @pl.when(tab_ref[2, pair] == 1)
    def _():
        m_ref[...] = jnp.full(m_ref.shape, NEG_INF, F32)
        l_ref[...] = jnp.zeros(l_ref.shape, F32)
        acc_ref[...] = jnp.zeros(acc_ref.shape, F32)

    def scores(hh):
        q = q_ref[0, :, hh * w:(hh + 1) * w]
        if rep == 1:
            qs = q.astype(BF)
        else:
            qs = jnp.concatenate([q[:, r * HEAD_DIM:(r + 1) * HEAD_DIM] for r in range(rep)], axis=0).astype(BF)
        return _nt(k_ref[0, :, hh * HEAD_DIM:(hh + 1) * HEAD_DIM].astype(BF), qs) * SCALE

    def causal():
        ki = lax.broadcasted_iota(jnp.int32, (tq, tq), 0)
        qj = lax.broadcasted_iota(jnp.int32, (tq, tq), 1)
        return (qi - tile) * tq + qj - ki

    def update(hh, s, mask):
        if mask is not None:
            s = jnp.where(mask, s, NEG_INF)
        m_old = m_ref[hh]
        m_new = jnp.maximum(m_old, jnp.max(s, axis=0, keepdims=True))
        p = jnp.exp(s - m_new)
        if mask is not None:
            p = p * mask.astype(F32)
        alpha = jnp.exp(m_old - m_new)
        l_ref[hh] = l_ref[hh] * alpha + jnp.sum(p, axis=0, keepdims=True)
        acc_ref[hh] = acc_ref[hh] * alpha + _tn(v_ref[0, :, hh * HEAD_DIM:(hh + 1) * HEAD_DIM].astype(BF), p.astype(BF))
        m_ref[hh] = m_new

    if mode == "fox":
        ss = []
        for hh in range(hps):
            fk = _lane_col(fcol_ref[0], FB_LANE + hg0 + hh)
            ss.append(scores(hh) + (frow_ref[0, pl.ds(hg0 + hh, 1), :] - fk))

        @pl.when(tile == qi)
        def _():
            mask = causal() >= 0
            for hh in range(hps):
                update(hh, ss[hh], mask)

        @pl.when(tile != qi)
        def _():
            for hh in range(hps):
                update(hh, ss[hh], None)
    elif mode == "win":
        ss = [scores(hh) + bias_ref[0, hh] for hh in range(hps)]
        inner = (tile < qi) & ((qi - tile + 1) * tq - 1 <= WINDOW)

        @pl.when(inner)
        def _():
            for hh in range(hps):
                update(hh, ss[hh], None)

        @pl.when(jnp.logical_not(inner))
        def _():
            dist = causal()
            mask = jnp.concatenate([(dist >= 0) & (dist <= WINDOW)] * rep, axis=1)
            for hh in range(hps):
                update(hh, ss[hh], mask)
    else:
        dist = causal()
        key_blk = (tile * tq + lax.broadcasted_iota(jnp.int32, (tq, LANES), 0)) // SEL_BLOCK
        expand = (key_blk == lax.broadcasted_iota(jnp.int32, (tq, LANES), 1)).astype(BF)
        for hh in range(hps):
            hit = _dot(expand, sel_ref[0, hh].astype(BF))
            mask = (dist >= 0) & (hit > 0.5)
            update(hh, scores(hh) + bias_ref[0, hh], jnp.concatenate([mask] * rep, axis=1))

    @pl.when(tab_ref[3, pair] == 1)
    def _():
        for hh in range(hps):
            o = acc_ref[hh] / jnp.maximum(l_ref[hh], 1e-30)
            if mode == "fox":
                o_ref[0, :, hh * HEAD_DIM:(hh + 1) * HEAD_DIM] = o.T
            else:
                ga = ga_ref[0]
                branch = 1 if mode == "sel" else 2
                for r in range(rep):
                    hd = (hg0 + hh) * NSA_REP + r
                    gate = _lane_col(ga, hd * 3 + branch)
                    o_ref[0, :, (hh * rep + r) * HEAD_DIM:(hh * rep + r + 1) * HEAD_DIM] = (
                        o[:, r * tq:(r + 1) * tq].T * gate)


def _flash_tile(mode, t):
    return min(FOX_T if mode == "fox" else FLASH_T, t)


def _flash(mode, z, aux):
    n, t, _ = z.shape
    tq = _flash_tile(mode, t)
    nq = t // tq
    back = WINDOW // tq if mode == "win" else nq
    pairs = [(i, k) for i in range(nq) for k in range(max(i - back, 0), i + 1)]
    tab = jnp.asarray(np.array(
        [[i for i, _ in pairs], [k for _, k in pairs],
         [int(k == max(i - back, 0)) for i, k in pairs], [int(k == i) for i, k in pairs]], np.int32))
    if mode == "fox":
        heads, rep = FOX_HEADS, 1
        qc, kc, vc = QB0 // HEAD_DIM, (QB0 + FOX_WIDTH) // HEAD_DIM, (QB0 + 2 * FOX_WIDTH) // HEAD_DIM
    else:
        heads, rep = NSA_KV, NSA_REP
        off = 2 if mode == "sel" else 4
        qc = QA0 // (rep * HEAD_DIM)
        kc = KV0 // HEAD_DIM + off * NSA_KV
        vc = KV0 // HEAD_DIM + (off + 1) * NSA_KV
    w = rep * HEAD_DIM
    hps = FLASH_HPS
    assert heads % hps == 0 and qc % hps == 0 and kc % hps == 0 and vc % hps == 0
    qc, kc, vc = qc // hps, kc // hps, vc // hps
    in_specs = [pl.BlockSpec((1, tq, hps * w), lambda b, h, p, tb: (b, tb[0, p], qc + h)),
                pl.BlockSpec((1, tq, hps * HEAD_DIM), lambda b, h, p, tb: (b, tb[1, p], kc + h)),
                pl.BlockSpec((1, tq, hps * HEAD_DIM), lambda b, h, p, tb: (b, tb[1, p], vc + h))]
    if mode == "fox":
        f, ft = aux
        args = (z, z, z, f, ft)
        in_specs += [pl.BlockSpec((1, tq, LANES), lambda b, h, p, tb: (b, tb[1, p], 0)),
                     pl.BlockSpec((1, FOX_HEADS, tq), lambda b, h, p, tb: (b, 0, tb[0, p]))]
    else:
        bias, ga = aux[0], aux[1]
        args = (z, z, z, bias, ga)
        in_specs += [pl.BlockSpec((1, hps, tq, rep * tq),
                                  lambda b, h, p, tb: (jnp.minimum(tb[0, p] - tb[1, p], 2), h, 0, 0)),
                     pl.BlockSpec((1, tq, LANES), lambda b, h, p, tb: (b, tb[0, p], 0))]
        if mode == "sel":
            args += (aux[2],)
            in_specs += [pl.BlockSpec((1, hps, LANES, tq), lambda b, h, p, tb: (b, h, 0, tb[0, p]))]
    gs = pltpu.PrefetchScalarGridSpec(
        num_scalar_prefetch=1, grid=(n, heads // hps, len(pairs)), in_specs=in_specs,
        out_specs=pl.BlockSpec((1, tq, hps * w), lambda b, h, p, tb: (b, tb[0, p], h)),
        scratch_shapes=[pltpu.VMEM((hps, 1, rep * tq), F32), pltpu.VMEM((hps, 1, rep * tq), F32),
                        pltpu.VMEM((hps, HEAD_DIM, rep * tq), F32)])
    return pl.pallas_call(
        functools.partial(_flash_body, mode=mode, tq=tq, hps=hps),
        grid_spec=gs,
        out_shape=jax.ShapeDtypeStruct((n, t, heads * w), F32),
        compiler_params=_params(("parallel", "parallel", "arbitrary"), 48),
        name="flash_" + mode,
    )(tab, *args)


def _merge1_body(oc_ref, os_ref, ow_ref, ob_ref, ga_ref, gb_ref, wa_ref, wb_ref, u_ref, oa_s, ob_s):
    @pl.when(pl.program_id(1) == 0)
    def _():
        oa_s[...] = (oc_ref[...] + os_ref[...] + ow_ref[...]).astype(BF)
        ob_s[...] = ob_ref[...].astype(BF)

    ya = _dot(oa_s[...], wa_ref[...])
    yb = _dot(ob_s[...], wb_ref[...])
    u_ref[...] = (jax.nn.sigmoid(ga_ref[...]) * ya + jax.nn.sigmoid(gb_ref[...]) * yb).astype(BF)


def _merge1(oc, osel, ow, ob, z2d, wa, wb):
    rows = oc.shape[0]
    tm, tn = 512, 512
    ospec = pl.BlockSpec((tm, NSA_WIDTH), lambda i, j: (i, 0))
    return pl.pallas_call(
        _merge1_body,
        grid=(rows // tm, D_MODEL // tn),
        in_specs=[ospec, ospec, ospec, ospec,
                  pl.BlockSpec((tm, tn), lambda i, j: (i, GM0 // tn + j)),
                  pl.BlockSpec((tm, tn), lambda i, j: (i, (GM0 + D_MODEL) // tn + j)),
                  pl.BlockSpec((NSA_WIDTH, tn), lambda i, j: (0, j)),
                  pl.BlockSpec((FOX_WIDTH, tn), lambda i, j: (0, j))],
        out_specs=pl.BlockSpec((tm, tn), lambda i, j: (i, j)),
        out_shape=jax.ShapeDtypeStruct((rows, D_MODEL), BF),
        scratch_shapes=[pltpu.VMEM((tm, NSA_WIDTH), BF), pltpu.VMEM((tm, FOX_WIDTH), BF)],
        compiler_params=_params(("parallel", "arbitrary"), 48),
        name="merge1",
    )(oc, osel, ow, ob, z2d, z2d, wa, wb)


def _layer_norm(y, g, b):
    mu = jnp.mean(y, axis=-1, keepdims=True)
    yc = y - mu
    var = jnp.mean(yc * yc, axis=-1, keepdims=True)
    return yc * lax.rsqrt(var + LN_EPS) * g + b


def _merge2_body(u_ref, wo_ref, x_ref, gt_ref, g_ref, b_ref, o_ref):
    mix = _dot(u_ref[0], wo_ref[...])
    y = DEEPNORM_ALPHA * x_ref[0] + gt_ref[0] * mix
    o_ref[0] = _layer_norm(y, g_ref[...], b_ref[...])


def _merge2(u, wo, x, gate, g, b):
    n, t, d = x.shape
    tm = min(512, t)
    ts = gate.shape[1]
    gspec = (pl.BlockSpec((1, tm, d), lambda bb, i: (bb, i, 0)) if ts == t
             else pl.BlockSpec((1, 1, d), lambda bb, i: (bb, 0, 0)))
    vec = pl.BlockSpec((1, d), lambda bb, i: (0, 0))
    return pl.pallas_call(
        _merge2_body,
        grid=(n, t // tm),
        in_specs=[pl.BlockSpec((1, tm, d), lambda bb, i: (bb, i, 0)),
                  pl.BlockSpec((d, d), lambda bb, i: (0, 0)),
                  pl.BlockSpec((1, tm, d), lambda bb, i: (bb, i, 0)),
                  gspec, vec, vec],
        out_specs=pl.BlockSpec((1, tm, d), lambda bb, i: (bb, i, 0)),
        out_shape=jax.ShapeDtypeStruct((n, t, d), F32),
        compiler_params=_params(("parallel", "parallel"), 56),
        name="merge2",
    )(u, wo, x, gate, g, b)


def _merge_desc(lst):
    n = len(lst)
    j = n // 2
    while j >= 1:
        for i in range(n):
            l = i ^ j
            if l > i:
                lst[i], lst[l] = jnp.maximum(lst[i], lst[l]), jnp.minimum(lst[i], lst[l])
        j //= 2
    return lst


def _sort_desc(lst):
    n = len(lst)
    k = 2
    while k <= n:
        j = k // 2
        while j >= 1:
            for i in range(n):
                l = i ^ j
                if l > i:
                    hi, lo = jnp.maximum(lst[i], lst[l]), jnp.minimum(lst[i], lst[l])
                    lst[i], lst[l] = (hi, lo) if (i & k) == 0 else (lo, hi)
            j //= 2
        k *= 2
    return lst


def _top_across_sublanes(lst):
    n = len(lst)
    for sh in (SUB // 2, SUB // 4, SUB // 8):
        part = [pltpu.roll(x, sh, 0) for x in lst]
        lst = _merge_desc([jnp.maximum(lst[i], part[n - 1 - i]) for i in range(n)])
    return lst


def _route_body(q_ref, sk_ref, a_ref, b_ref, ea_ref, eb_ref, tau_ref, *, tt):
    half = PEER_DK // 2
    k = PEER_TOPK
    groups = PEER_NKEYS // SUB

    def head(h, tau_all):
        r0 = pl.multiple_of(h * PEER_NKEYS, PEER_NKEYS)
        c0 = pl.multiple_of(h * PEER_DK, PEER_DK)
        a_ref[0, pl.ds(r0, PEER_NKEYS), :] = _nt(sk_ref[h, 0], q_ref[:, pl.ds(c0, half)].astype(BF))
        b_ref[0, pl.ds(r0, PEER_NKEYS), :] = _nt(sk_ref[h, 1], q_ref[:, pl.ds(c0 + half, half)].astype(BF))
        sub = lax.broadcasted_iota(jnp.int32, (SUB, LANES), 0)
        taus = []
        for c in range(tt // LANES):
            lanes = slice(c * LANES, (c + 1) * LANES)
            tops = []
            for ref in (a_ref, b_ref):
                rows = [ref[0, pl.ds(r0 + SUB * r, SUB), lanes] for r in range(groups)]
                tops.append(_top_across_sublanes(_sort_desc(rows)))
            t1, t2 = tops
            best = None
            for kk in range(k // SUB):
                spread = t2[SUB * kk]
                for s in range(1, SUB):
                    spread = jnp.where(sub == s, t2[SUB * kk + s], spread)
                cand = [t1[r] + spread for r in range(k)]
                best = cand if best is None else _merge_desc([jnp.maximum(best[i], cand[k - 1 - i]) for i in range(k)])
            tv = _top_across_sublanes(best)
            zsum = jnp.exp(tv[0] - tv[0])
            for i in range(1, k):
                zsum = zsum + jnp.exp(tv[i] - tv[0])
            s1 = a_ref[0, pl.ds(r0, PEER_NKEYS), lanes]
            s2 = b_ref[0, pl.ds(r0, PEER_NKEYS), lanes]
            ea_ref[0, pl.ds(r0, PEER_NKEYS), lanes] = jnp.exp(s1 - t1[0][0:1]) / zsum[0:1]
            eb_ref[0, pl.ds(r0, PEER_NKEYS), lanes] = jnp.exp(s2 - t2[0][0:1])
            taus.append(tv[k - 1])
        head_row = lax.broadcasted_iota(jnp.int32, (PEER_HEADS, tt), 0) == h
        return jnp.where(head_row, jnp.concatenate(taus, axis=1), tau_all)

    tau_ref[0] = lax.fori_loop(0, PEER_HEADS, head, jnp.zeros((PEER_HEADS, tt), F32))


def _route(q2d, sk):
    rows = q2d.shape[0]
    tt = PEER_TOK
    nblk = rows // tt
    big = jax.ShapeDtypeStruct((nblk, PEER_HEADS * PEER_NKEYS, tt), F32)
    bspec = pl.BlockSpec((1, PEER_HEADS * PEER_NKEYS, tt), lambda i: (i, 0, 0))
    return pl.pallas_call(
        functools.partial(_route_body, tt=tt),
        grid=(nblk,),
        in_specs=[pl.BlockSpec((tt, PEER_HEADS * PEER_DK), lambda i: (i, 0)),
                  pl.BlockSpec(sk.shape, lambda i: (0, 0, 0, 0))],
        out_specs=[bspec, bspec, bspec, bspec, pl.BlockSpec((1, PEER_HEADS, tt), lambda i: (i, 0, 0))],
        out_shape=[big, big, big, big, jax.ShapeDtypeStruct((nblk, PEER_HEADS, tt), F32)],
        compiler_params=_params(("parallel",), 48),
        name="route",
    )(q2d, sk)


def _peer_body(h_ref, u_ref, v_ref, a_ref, b_ref, ea_ref, eb_ref, tau_ref, x_ref, gt_ref, g_ref, bb_ref,
               o_ref, acc_ref, pre_ref, act_ref, *, te, tt):
    j = pl.program_id(1)
    nj = pl.num_programs(1) - 1
    slab = 32
    cur = j % 2

    @pl.when(j == 0)
    def _():
        acc_ref[...] = jnp.zeros(acc_ref.shape, F32)
        act_ref[1] = jnp.zeros(act_ref.shape[1:], BF)

    tile = jnp.minimum(j, nj - 1)
    acc_ref[...] += _tn(act_ref[1 - cur], v_ref[...])
    pre_ref[...] = _nt(u_ref[...], h_ref[...])
    for sidx in range(te // slab):
        a = tile * (te // PEER_NKEYS) + (sidx * slab) // PEER_NKEYS
        b0 = (sidx * slab) % PEER_NKEYS
        gm = None
        for h in range(PEER_HEADS):
            arow = a_ref[0, pl.ds(h * PEER_NKEYS + a, 1), :]
            earow = ea_ref[0, pl.ds(h * PEER_NKEYS + a, 1), :]
            bs = b_ref[0, h * PEER_NKEYS + b0:h * PEER_NKEYS + b0 + slab, :]
            ebs = eb_ref[0, h * PEER_NKEYS + b0:h * PEER_NKEYS + b0 + slab, :]
            term = jnp.where(arow + bs >= tau_ref[0, h:h + 1, :], earow * ebs, 0.0)
            gm = term if gm is None else gm + term
        pre = pre_ref[sidx * slab:(sidx + 1) * slab, :]
        act_ref[cur, sidx * slab:(sidx + 1) * slab, :] = (jax.nn.gelu(pre) * gm).astype(BF)

    @pl.when(j == nj)
    def _():
        y = DEEPNORM_ALPHA * x_ref[...] + gt_ref[0] * acc_ref[...]
        o_ref[...] = _layer_norm(y, g_ref[...], bb_ref[...])


def _peer(h2, u, v, a, b, ea, eb, tau, x1, gate, g, bb, rows_per_gate):
    rows, d = x1.shape
    tt, te = PEER_TOK, PEER_ETILE
    nblk = rows // tt
    nj = PEER_EXPERTS // te
    one = pl.Buffered(1)
    rspec = pl.BlockSpec((1, PEER_HEADS * PEER_NKEYS, tt), lambda i, j: (i, 0, 0), pipeline_mode=one)
    if rows_per_gate == 1:
        gspec = pl.BlockSpec((1, tt, d), lambda i, j: (i, 0, 0), pipeline_mode=one)
    else:
        gspec = pl.BlockSpec((1, 1, d), lambda i, j: (i * tt // rows_per_gate, 0, 0))
    vec = pl.BlockSpec((1, d), lambda i, j: (0, 0))
    return pl.pallas_call(
        functools.partial(_peer_body, te=te, tt=tt),
        grid=(nblk, nj + 1),
        in_specs=[pl.BlockSpec((tt, d), lambda i, j: (i, 0), pipeline_mode=one),
                  pl.BlockSpec((te, d), lambda i, j: (jnp.minimum(j, nj - 1), 0)),
                  pl.BlockSpec((te, d), lambda i, j: (jnp.maximum(j - 1, 0), 0)),
                  rspec, rspec, rspec, rspec,
                  pl.BlockSpec((1, PEER_HEADS, tt), lambda i, j: (i, 0, 0)),
                  pl.BlockSpec((tt, d), lambda i, j: (i, 0), pipeline_mode=one),
                  gspec, vec, vec],
        out_specs=pl.BlockSpec((tt, d), lambda i, j: (i, 0)),
        out_shape=jax.ShapeDtypeStruct((rows, d), F32),
        scratch_shapes=[pltpu.VMEM((tt, d), F32), pltpu.VMEM((te, tt), F32), pltpu.VMEM((2, te, tt), BF)],
        compiler_params=_params(("parallel", "arbitrary"), 56),
        name="peer",
    )(h2, u, v, a, b, ea, eb, tau, x1, gate, g, bb)


def _pad_rows(x, rows):
    return jnp.concatenate([x, jnp.zeros((rows - x.shape[0], x.shape[1]), x.dtype)], axis=0)


def _nsa_s_body(pt_ref, *refs, npg):
    del pt_ref
    kp = refs[:npg]
    vp = refs[npg:2 * npg]
    (z_ref, ga_ref, w1_ref, pos_ref, w2_ref, bc_ref, bs_ref, ovl_ref, oc_ref, os_ref, xc_ref) = refs[2 * npg:]
    past = npg * LANES
    nb = past // CMP_STRIDE - 1
    ns = -(-(past + 4) // SEL_BLOCK)
    rows = SAMPLE_ROWS
    ga = ga_ref[0]
    tpos = past + lax.broadcasted_iota(jnp.int32, (rows, 1), 0)
    rowi = lax.broadcasted_iota(jnp.int32, (rows, LANES), 0)
    lane = lax.broadcasted_iota(jnp.int32, (rows, LANES), 1)
    cpp = LANES // CMP_STRIDE
    rstride = 2 * NSA_KV
    for g in range(NSA_KV):
        kvc = []
        for kind, pages in enumerate((kp, vp)):
            for p in range(npg):
                for s in range(CMP_STRIDE):
                    xc_ref[p * cpp:(p + 1) * cpp, s * HEAD_DIM:(s + 1) * HEAD_DIM] = (
                        pages[p][0, pl.ds(rstride * s + g, cpp, stride=rstride * CMP_STRIDE), :])
            kvc.append(_compress_core(xc_ref[...].astype(BF), w1_ref, pos_ref, w2_ref, kind))
        qs = [z_ref[0, :, QA0 + (g * NSA_REP + r) * HEAD_DIM:QA0 + (g * NSA_REP + r + 1) * HEAD_DIM].astype(BF)
              for r in range(NSA_REP)]
        outs, sel = _cmp_sel_math(qs, kvc[0], kvc[1], bc_ref[g * NSA_REP:(g + 1) * NSA_REP], tpos, nb, ns,
                                  ovl_ref[...])
        for r in range(NSA_REP):
            hd = g * NSA_REP + r
            oc_ref[0, :, hd * HEAD_DIM:(hd + 1) * HEAD_DIM] = outs[r] * ga[:, hd * 3:hd * 3 + 1]
        qst = jnp.concatenate(qs, axis=0)
        ks0 = KV0 + 2 * NSA_KV * HEAD_DIM + g * HEAD_DIM
        vs0 = KV0 + 3 * NSA_KV * HEAD_DIM + g * HEAD_DIM
        knew = _pad_rows(z_ref[0, :, ks0:ks0 + HEAD_DIM], LANES).astype(BF)
        vnew = _pad_rows(z_ref[0, :, vs0:vs0 + HEAD_DIM], LANES).astype(BF)
        bpp = LANES // SEL_BLOCK
        s_tiles, m_tiles = [], []
        for p in range(npg + 1):
            if p < npg:
                kt = kp[p][0, pl.ds(NSA_KV + g, LANES, stride=rstride), :].astype(BF)
                mk = jnp.zeros((rows, LANES), F32)
                for bq in range(bpp):
                    in_blk = (lane >= bq * SEL_BLOCK) & (lane < (bq + 1) * SEL_BLOCK)
                    mk = jnp.where(in_blk, sel[:, p * bpp + bq:p * bpp + bq + 1], mk)
                mk = mk > 0.5
            else:
                kt = knew
                mk = (lane <= rowi) & (lane < rows) & (sel[:, ns - 1:ns] > 0.5)
            bias = bs_ref[g * NSA_REP:(g + 1) * NSA_REP, :, p * LANES:(p + 1) * LANES].reshape(NSA_REP * rows, LANES)
            mk = jnp.concatenate([mk] * NSA_REP, axis=0)
            s_tiles.append(jnp.where(mk, _nt(qst, kt) * SCALE + bias, NEG_INF))
            m_tiles.append(mk)
        mx = s_tiles[0].max(axis=-1, keepdims=True)
        for st in s_tiles[1:]:
            mx = jnp.maximum(mx, st.max(axis=-1, keepdims=True))
        den = jnp.zeros((NSA_REP * rows, 1), F32)
        acc = jnp.zeros((NSA_REP * rows, HEAD_DIM), F32)
        for p in range(npg + 1):
            pr = jnp.exp(s_tiles[p] - mx) * m_tiles[p].astype(F32)
            den = den + jnp.sum(pr, axis=-1, keepdims=True)
            if p < npg:
                vt = vp[p][0, pl.ds(NSA_KV + g, LANES, stride=rstride), :].astype(BF)
            else:
                vt = vnew
            acc = acc + _dot(pr.astype(BF), vt)
        o = acc / jnp.maximum(den, 1e-30)
        for r in range(NSA_REP):
            hd = g * NSA_REP + r
            os_ref[0, :, hd * HEAD_DIM:(hd + 1) * HEAD_DIM] = o[r * rows:(r + 1) * rows] * ga[:, hd * 3 + 1:hd * 3 + 2]


def _nsa_s(pt, ck, cv, zs, ga, w1, pos, w2, bias_c, bias_s, ovl):
    bsz, npg = pt.shape
    rows = SAMPLE_ROWS
    prow = ck.shape[1]
    page = lambda p: pl.BlockSpec((1, prow, HEAD_DIM), functools.partial(lambda b, t, p: (t[b, p], 0, 0), p=p))
    const = lambda shp: pl.BlockSpec(shp, lambda b, t: (0,) * len(shp))
    in_specs = ([page(p) for p in range(npg)] + [page(p) for p in range(npg)]
                + [pl.BlockSpec((1, rows, PROJ_COLS), lambda b, t: (b, 0, 0)),
                   pl.BlockSpec((1, rows, LANES), lambda b, t: (b, 0, 0)),
                   const(w1.shape), const(pos.shape), const(w2.shape), const(bias_c.shape), const(bias_s.shape),
                   const(ovl.shape)])
    ospec = pl.BlockSpec((1, rows, NSA_WIDTH), lambda b, t: (b, 0, 0))
    gs = pltpu.PrefetchScalarGridSpec(
        num_scalar_prefetch=1, grid=(bsz,), in_specs=in_specs, out_specs=[ospec, ospec],
        scratch_shapes=[pltpu.VMEM((npg * LANES // CMP_STRIDE, CMP_STRIDE * HEAD_DIM), F32)])
    sds = jax.ShapeDtypeStruct((bsz, rows, NSA_WIDTH), F32)
    return pl.pallas_call(
        functools.partial(_nsa_s_body, npg=npg),
        grid_spec=gs, out_shape=[sds, sds],
        compiler_params=_params(("parallel",), 56),
        name="nsa_s",
    )(pt, *([ck] * npg), *([cv] * npg), zs, ga, w1, pos, w2, bias_c, bias_s, ovl)


def _win_s_body(sk_ref, sv_ref, z_ref, ga_ref, bw_ref, o_ref, nk_ref, nv_ref, *, wbuf, nnew):
    rows = SAMPLE_ROWS
    ga = ga_ref[0]
    rowi = lax.broadcasted_iota(jnp.int32, (rows, LANES), 0)
    lane = lax.broadcasted_iota(jnp.int32, (rows, LANES), 1)
    rowk = lax.broadcasted_iota(jnp.int32, (rows, wbuf), 0)
    colk = lax.broadcasted_iota(jnp.int32, (rows, wbuf), 1)
    dist_old = wbuf + rowk - colk
    m_old = jnp.concatenate([(dist_old >= 0) & (dist_old <= WINDOW)] * NSA_REP, axis=0)
    m_new = jnp.concatenate([(lane <= rowi) & (lane < rows) & (rowi - lane <= WINDOW)] * NSA_REP, axis=0)
    for g in range(NSA_KV):
        qst = jnp.concatenate(
            [z_ref[0, :, QA0 + (g * NSA_REP + r) * HEAD_DIM:QA0 + (g * NSA_REP + r + 1) * HEAD_DIM]
             for r in range(NSA_REP)], axis=0).astype(BF)
        kw0 = KV0 + 4 * NSA_KV * HEAD_DIM + g * HEAD_DIM
        vw0 = KV0 + 5 * NSA_KV * HEAD_DIM + g * HEAD_DIM
        kold = sk_ref[0, pl.ds(g, wbuf, stride=NSA_KV), :].astype(BF)
        vold = sv_ref[0, pl.ds(g, wbuf, stride=NSA_KV), :].astype(BF)
        knew = _pad_rows(z_ref[0, :, kw0:kw0 + HEAD_DIM], LANES).astype(BF)
        vnew = _pad_rows(z_ref[0, :, vw0:vw0 + HEAD_DIM], LANES).astype(BF)
        b_old = bw_ref[g * NSA_REP:(g + 1) * NSA_REP, :, 0:wbuf].reshape(NSA_REP * rows, wbuf)
        b_new = bw_ref[g * NSA_REP:(g + 1) * NSA_REP, :, wbuf:wbuf + LANES].reshape(NSA_REP * rows, LANES)
        s_old = jnp.where(m_old, _nt(qst, kold) * SCALE + b_old, NEG_INF)
        s_new = jnp.where(m_new, _nt(qst, knew) * SCALE + b_new, NEG_INF)
        mx = jnp.maximum(s_old.max(axis=-1, keepdims=True), s_new.max(axis=-1, keepdims=True))
        p_old = jnp.exp(s_old - mx) * m_old.astype(F32)
        p_new = jnp.exp(s_new - mx) * m_new.astype(F32)
        den = jnp.sum(p_old, axis=-1, keepdims=True) + jnp.sum(p_new, axis=-1, keepdims=True)
        o = (_dot(p_old.astype(BF), vold) + _dot(p_new.astype(BF), vnew)) / jnp.maximum(den, 1e-30)
        for r in range(NSA_REP):
            hd = g * NSA_REP + r
            o_ref[0, :, hd * HEAD_DIM:(hd + 1) * HEAD_DIM] = o[r * rows:(r + 1) * rows] * ga[:, hd * 3 + 2:hd * 3 + 3]
    keep = (wbuf - nnew) * NSA_KV
    nk_ref[0, 0:keep, :] = sk_ref[0, nnew * NSA_KV:wbuf * NSA_KV, :]
    nv_ref[0, 0:keep, :] = sv_ref[0, nnew * NSA_KV:wbuf * NSA_KV, :]
    for (c0, dst) in ((KV0 + 4 * NSA_KV * HEAD_DIM, nk_ref), (KV0 + 5 * NSA_KV * HEAD_DIM, nv_ref)):
        new = jnp.concatenate(
            [z_ref[0, i:i + 1, c0 + g * HEAD_DIM:c0 + (g + 1) * HEAD_DIM] for i in range(nnew) for g in range(NSA_KV)],
            axis=0)
        dst[0, keep:wbuf * NSA_KV, :] = new


def _win_s(st_k, st_v, layer, zs, ga, bias_w, nnew):
    bsz = zs.shape[0]
    rows = SAMPLE_ROWS
    srows = st_k.shape[1]
    wbuf = srows // NSA_KV
    sspec = pl.BlockSpec((1, srows, HEAD_DIM), lambda b: (layer * bsz + b, 0, 0))
    nspec = pl.BlockSpec((1, srows, HEAD_DIM), lambda b: (b, 0, 0))
    nsds = jax.ShapeDtypeStruct((bsz, srows, HEAD_DIM), F32)
    return pl.pallas_call(
        functools.partial(_win_s_body, wbuf=wbuf, nnew=nnew),
        grid=(bsz,),
        in_specs=[sspec, sspec,
                  pl.BlockSpec((1, rows, PROJ_COLS), lambda b: (b, 0, 0)),
                  pl.BlockSpec((1, rows, LANES), lambda b: (b, 0, 0)),
                  pl.BlockSpec(bias_w.shape, lambda b: (0, 0, 0))],
        out_specs=[pl.BlockSpec((1, rows, NSA_WIDTH), lambda b: (b, 0, 0)), nspec, nspec],
        out_shape=[jax.ShapeDtypeStruct((bsz, rows, NSA_WIDTH), F32), nsds, nsds],
        compiler_params=_params(("parallel",), 32),
        name="win_s",
    )(st_k, st_v, zs, ga, bias_w)


def _fox_s_body(pt_ref, *refs, npg):
    del pt_ref
    kp = refs[:npg]
    vp = refs[npg:2 * npg]
    fp = refs[2 * npg:3 * npg]
    z_ref, lf_ref, o_ref = refs[3 * npg:]
    rows = SAMPLE_ROWS
    rowi = lax.broadcasted_iota(jnp.int32, (rows, LANES), 0)
    lane = lax.broadcasted_iota(jnp.int32, (rows, LANES), 1)
    r128 = lax.broadcasted_iota(jnp.int32, (LANES, LANES), 0)
    c128 = lax.broadcasted_iota(jnp.int32, (LANES, LANES), 1)
    upper = (r128 <= c128).astype(BF)
    ft = []
    carry = jnp.zeros((LANES, 1), F32)
    for p in range(npg):
        lp = fp[p][0]
        lpad = jnp.concatenate([lp, jnp.zeros((LANES, LANES - FOX_HEADS), F32)], axis=1)
        hi, mid, lo = _split3(lpad)
        cs = _tn(hi, upper) + _tn(mid, upper) + _tn(lo, upper) + carry
        carry = cs[:, LANES - 1:LANES]
        ft.append(cs[0:FOX_HEADS])
    lf_new = lf_ref[0]
    x = lf_new
    for sh in (1, 2, 4):
        x = x + jnp.where(rowi >= sh, pltpu.roll(x, sh, 0), 0.0)
    tot_t = jnp.broadcast_to(carry, (LANES, LANES)).T
    tot_row = pltpu.roll(tot_t[0:1, :], FB_LANE, 1)
    f_new = x + tot_row
    f_new_t = _pad_rows(f_new, LANES).T
    m_new = (lane <= rowi) & (lane < rows)
    for h in range(FOX_HEADS):
        q = z_ref[0, :, QB0 + h * HEAD_DIM:QB0 + (h + 1) * HEAD_DIM].astype(BF)
        k0 = QB0 + FOX_WIDTH + h * HEAD_DIM
        v0 = QB0 + 2 * FOX_WIDTH + h * HEAD_DIM
        knew = _pad_rows(z_ref[0, :, k0:k0 + HEAD_DIM], LANES).astype(BF)
        vnew = _pad_rows(z_ref[0, :, v0:v0 + HEAD_DIM], LANES).astype(BF)
        fq = f_new[:, FB_LANE + h:FB_LANE + h + 1]
        s_tiles = []
        for p in range(npg):
            kt = kp[p][0, pl.ds(h, LANES, stride=FOX_HEADS), :].astype(BF)
            s_tiles.append(_nt(q, kt) * SCALE + (fq - ft[p][h:h + 1, :]))
        s_n = _nt(q, knew) * SCALE + (fq - f_new_t[FB_LANE + h:FB_LANE + h + 1, :])
        s_tiles.append(jnp.where(m_new, s_n, NEG_INF))
        mx = s_tiles[0].max(axis=-1, keepdims=True)
        for st in s_tiles[1:]:
            mx = jnp.maximum(mx, st.max(axis=-1, keepdims=True))
        den = jnp.zeros((rows, 1), F32)
        acc = jnp.zeros((rows, HEAD_DIM), F32)
        for p in range(npg + 1):
            pr = jnp.exp(s_tiles[p] - mx)
            if p == npg:
                pr = pr * m_new.astype(F32)
                vt = vnew
            else:
                vt = vp[p][0, pl.ds(h, LANES, stride=FOX_HEADS), :].astype(BF)
            den = den + jnp.sum(pr, axis=-1, keepdims=True)
            acc = acc + _dot(pr.astype(BF), vt)
        o_ref[0, :, h * HEAD_DIM:(h + 1) * HEAD_DIM] = acc / jnp.maximum(den, 1e-30)


def _fox_s(pt, ck, cv, cf, zs, lf):
    bsz, npg = pt.shape
    rows = SAMPLE_ROWS
    prow = ck.shape[1]
    pmap = lambda p: functools.partial(lambda b, t, p: (t[b, p], 0, 0), p=p)
    in_specs = ([pl.BlockSpec((1, prow, HEAD_DIM), pmap(p)) for p in range(npg)]
                + [pl.BlockSpec((1, prow, HEAD_DIM), pmap(p)) for p in range(npg)]
                + [pl.BlockSpec((1, LANES, FOX_HEADS), pmap(p)) for p in range(npg)]
                + [pl.BlockSpec((1, rows, PROJ_COLS), lambda b, t: (b, 0, 0)),
                   pl.BlockSpec((1, rows, LANES), lambda b, t: (b, 0, 0))])
    gs = pltpu.PrefetchScalarGridSpec(
        num_scalar_prefetch=1, grid=(bsz,), in_specs=in_specs,
        out_specs=pl.BlockSpec((1, rows, FOX_WIDTH), lambda b, t: (b, 0, 0)))
    return pl.pallas_call(
        functools.partial(_fox_s_body, npg=npg),
        grid_spec=gs, out_shape=jax.ShapeDtypeStruct((bsz, rows, FOX_WIDTH), F32),
        compiler_params=_params(("parallel",), 56),
        name="fox_s",
    )(pt, *([ck] * npg), *([cv] * npg), *([cf] * npg), zs, lf)


def _rel_bucket(dist):
    n = jnp.maximum(dist, 0)
    exact = REL_BUCKETS // 2
    scaled = jnp.log(jnp.maximum(n, 1).astype(F32) / exact) / math.log(REL_MAX_DIST / exact)
    large = jnp.minimum(exact + (scaled * (REL_BUCKETS - exact)).astype(jnp.int32), REL_BUCKETS - 1)
    return jnp.where(n < exact, n, large)


def _bias_of(rel_table, dist):
    bucket = _rel_bucket(dist)
    out = jnp.zeros((rel_table.shape[1],) + dist.shape, F32)
    for b in range(REL_BUCKETS):
        row = rel_table[b].reshape((-1,) + (1,) * dist.ndim)
        out = jnp.where(bucket[None] == b, row, out)
    return out


def _overlap(nb, ns):
    i = np.arange(LANES)[:, None]
    j = np.arange(LANES)[None, :]
    rs = SEL_BLOCK // CMP_STRIDE
    rc = CMP_LEN // CMP_STRIDE
    m = np.zeros((LANES, LANES), np.float32)
    for a in range(rs):
        for b in range(rc):
            m = m + (i == j * rs + a + b - (rc - 1))
    m = m * (i < nb) * (j < ns)
    return jnp.asarray(m, BF)


def kernel(x_prompt, x_sample, cache_nsa_k, cache_nsa_v, cache_fox_k, cache_fox_v, cache_fox_logf,
           state_nsa_win_k, state_nsa_win_v, page_table, c_prompt, c_sample, w_ada, b_ada, w_in,
           b_forget, cmp_pos, cmp_w1, cmp_w2, rel_table, w_branch_a, w_branch_b, w_out, ln_g, ln_b,
           peer_wq, peer_subkeys, peer_u, peer_v):
    bp, seq, d = x_prompt.shape
    bs, dec = x_sample.shape[:2]
    depth, n_pool, page = cache_nsa_k.shape[:3]
    npg = page_table.shape[1]
    past = npg * page
    wbuf = state_nsa_win_k.shape[2]
    rows = SAMPLE_ROWS
    assert d == D_MODEL and page == LANES and depth == DEPTH and dec <= rows
    assert seq % FLASH_T == 0 or seq < FLASH_T

    ck_nsa = cache_nsa_k.reshape(depth * n_pool, page * 2 * NSA_KV, HEAD_DIM)
    cv_nsa = cache_nsa_v.reshape(depth * n_pool, page * 2 * NSA_KV, HEAD_DIM)
    ck_fox = cache_fox_k.reshape(depth * n_pool, page * FOX_HEADS, HEAD_DIM)
    cv_fox = cache_fox_v.reshape(depth * n_pool, page * FOX_HEADS, HEAD_DIM)
    cf_fox = cache_fox_logf.reshape(depth * n_pool, page, FOX_HEADS)
    st_k = state_nsa_win_k.reshape(depth * bs, wbuf * NSA_KV, HEAD_DIM)
    st_v = state_nsa_win_v.reshape(depth * bs, wbuf * NSA_KV, HEAD_DIM)

    tq = _flash_tile("sel", seq)
    ii = jnp.arange(tq)

    def tile_bias(dd):
        b = _bias_of(rel_table, dd * tq + ii[None, :] - ii[:, None])
        return b.reshape(NSA_KV, NSA_REP, tq, tq).transpose(0, 2, 1, 3).reshape(NSA_KV, tq, NSA_REP * tq)

    bias_tile = jnp.stack([tile_bias(dd) for dd in range(3)])
    nb_p = seq // CMP_STRIDE - 1
    ns_p = -(-seq // SEL_BLOCK)
    lane = jnp.arange(LANES)
    bias_cmp_p = _bias_of(rel_table, jnp.arange(seq)[:, None] - (lane * CMP_STRIDE + CMP_LEN - 1)[None, :])
    nb_s = past // CMP_STRIDE - 1
    ns_s = -(-(past + dec) // SEL_BLOCK)
    tq_s = past + jnp.arange(rows)
    bias_cmp_s = _bias_of(rel_table, tq_s[:, None] - (lane * CMP_STRIDE + CMP_LEN - 1)[None, :])
    bias_sel_s = _bias_of(rel_table, tq_s[:, None] - jnp.arange(past + LANES)[None, :])
    bias_win_s = _bias_of(rel_table, wbuf + jnp.arange(rows)[:, None] - jnp.arange(wbuf + LANES)[None, :])
    ovl_p = _overlap(nb_p, ns_p).T
    ovl_s = _overlap(nb_s, ns_s)

    c_all = jnp.concatenate([c_prompt, c_sample], axis=0)
    xp = x_prompt
    xs = x_sample
    outs_p = {k: [] for k in ("nsa_k", "nsa_v", "fox_k", "fox_v", "logf", "win_k", "win_v")}
    outs_s = {k: [] for k in ("nsa_k", "nsa_v", "fox_k", "fox_v", "logf", "win_k", "win_v")}
    gh = NSA_KV * HEAD_DIM

    for l in range(depth):
        wl = w_in[l]
        o_ga = NSA_WIDTH + 6 * gh
        o_qb = o_ga + 3 * NSA_HEADS
        o_fb = o_qb + 3 * FOX_WIDTH
        o_gm = o_fb + FOX_HEADS
        wcat = jnp.concatenate(
            [wl[:, :o_ga], wl[:, o_qb:o_fb], wl[:, o_gm:], wl[:, o_ga:o_qb], wl[:, o_fb:o_gm],
             jnp.zeros((d, PROJ_COLS - GF0 - 3 * NSA_HEADS - FOX_HEADS), F32)], axis=1).astype(BF)
        bfpad = jnp.zeros((1, LANES), F32).at[0, FB_LANE:FB_LANE + FOX_HEADS].set(b_forget[l])
        w1 = cmp_w1[l].astype(BF)
        pos = cmp_pos[l].reshape(2, 1, CMP_LEN * HEAD_DIM)
        w2 = cmp_w2[l].astype(BF)
        wa = w_branch_a[l].astype(BF)
        wb = w_branch_b[l].astype(BF)
        wo = w_out[l].astype(BF)
        wq = peer_wq[l].astype(BF)
        sk = peer_subkeys[l].astype(BF)
        pu = peer_u[l].astype(BF)
        pv = peer_v[l].astype(BF)
        g1, b1 = ln_g[l, 0][None], ln_b[l, 0][None]
        g2, b2 = ln_g[l, 1][None], ln_b[l, 1][None]
        pt_l = page_table + l * n_pool

        mod = _ada(c_all, w_ada[l], b_ada[l][None]).reshape(bp + bs, 6, d)
        mod_p = mod[:bp]
        mod_s = mod[bp:]

        z = _proj(xp, mod_p[:, 0:1], mod_p[:, 1:2], wcat, tm=min(1024, seq), tn=1024)
        ga, lf, fcum = _gates(z, bfpad)
        kvc = _compress_p(z, w1, pos, w2)
        o_cmp, sel = _cmp_p(z, kvc, bias_cmp_p, ga, ovl_p, nb_p, ns_p)
        o_sel = _flash("sel", z, (bias_tile, ga, sel))
        o_win = _flash("win", z, (bias_tile, ga))
        f_t = jnp.swapaxes(fcum[:, :, FB_LANE:FB_LANE + FOX_HEADS], 1, 2)
        o_b = _flash("fox", z, (fcum, f_t))
        nrow = bp * seq
        u = _merge1(o_cmp.reshape(nrow, -1), o_sel.reshape(nrow, -1), o_win.reshape(nrow, -1),
                    o_b.reshape(nrow, -1), z.reshape(nrow, -1), wa, wb)
        x1 = _merge2(u.reshape(bp, seq, d), wo, xp, mod_p[:, 2:3], g1, b1)
        q, h2 = _proj(x1, mod_p[:, 3:4], mod_p[:, 4:5], wq, tm=min(1024, seq), tn=1024, emit_h=True)
        ra, rb, rea, reb, tau = _route(q.reshape(nrow, -1), sk)
        xp = _peer(h2.reshape(nrow, d), pu, pv, ra, rb, rea, reb, tau, x1.reshape(nrow, d),
                   mod_p[:, 5:6], g2, b2, rows_per_gate=seq).reshape(bp, seq, d)
        keep = min(WINDOW, seq)

        def piece(i0, r0=0):
            return z[:, r0:, KV0 + i0 * gh:KV0 + (i0 + 1) * gh].reshape(bp, seq - r0, NSA_KV, HEAD_DIM)

        outs_p["nsa_k"].append(jnp.stack([piece(0), piece(2)], axis=2))
        outs_p["nsa_v"].append(jnp.stack([piece(1), piece(3)], axis=2))
        outs_p["fox_k"].append(z[:, :, QB0 + FOX_WIDTH:QB0 + 2 * FOX_WIDTH].reshape(bp, seq, FOX_HEADS, HEAD_DIM))
        outs_p["fox_v"].append(z[:, :, QB0 + 2 * FOX_WIDTH:QB0 + 3 * FOX_WIDTH].reshape(bp, seq, FOX_HEADS, HEAD_DIM))
        outs_p["logf"].append(lf[:, :, FB_LANE:FB_LANE + FOX_HEADS])
        outs_p["win_k"].append(piece(4, seq - keep))
        outs_p["win_v"].append(piece(5, seq - keep))

        xs8 = jnp.pad(xs, ((0, 0), (0, rows - dec), (0, 0)))
        mrow = lambda k: jnp.repeat(mod_s[:, k], rows, axis=0)[None]
        zs = _proj(xs8.reshape(1, bs * rows, d), mrow(0), mrow(1), wcat, tm=min(512, bs * rows), tn=1024)
        ga_s, lf_s, _ = _gates(zs, bfpad)
        zs = zs.reshape(bs, rows, PROJ_COLS)
        ga_s = ga_s.reshape(bs, rows, LANES)
        lf_s = lf_s.reshape(bs, rows, LANES)
        oc_s, os_s = _nsa_s(pt_l, ck_nsa, cv_nsa, zs, ga_s, w1, pos, w2, bias_cmp_s, bias_sel_s, ovl_s)
        ow_s, nk_s, nv_s = _win_s(st_k, st_v, l, zs, ga_s, bias_win_s, dec)
        ob_s = _fox_s(pt_l, ck_fox, cv_fox, cf_fox, zs, lf_s)
        srow = bs * rows
        u_s = _merge1(oc_s.reshape(srow, -1), os_s.reshape(srow, -1), ow_s.reshape(srow, -1),
                      ob_s.reshape(srow, -1), zs.reshape(srow, -1), wa, wb)
        x1_s = _merge2(u_s.reshape(1, srow, d), wo, xs8.reshape(1, srow, d), mrow(2), g1, b1)
        x1_c = x1_s.reshape(bs, rows, d)[:, :dec].reshape(1, bs * dec, d)
        crow = lambda k: jnp.repeat(mod_s[:, k], dec, axis=0)[None]
        q_s, h2_s = _proj(x1_c, crow(3), crow(4), wq, tm=min(512, bs * dec), tn=1024, emit_h=True)
        ra, rb, rea, reb, tau = _route(q_s.reshape(bs * dec, -1), sk)
        xs = _peer(h2_s.reshape(bs * dec, d), pu, pv, ra, rb, rea, reb, tau, x1_c.reshape(bs * dec, d),
                   crow(5).reshape(-1, PEER_TOK, d), g2, b2, rows_per_gate=1).reshape(bs, dec, d)
        kv_s = zs[:, :dec, KV0:KV0 + 6 * gh].reshape(bs, dec, 6, NSA_KV, HEAD_DIM)
        outs_s["nsa_k"].append(kv_s[:, :, 0:4:2])
        outs_s["nsa_v"].append(kv_s[:, :, 1:4:2])
        outs_s["fox_k"].append(zs[:, :dec, QB0 + FOX_WIDTH:QB0 + 2 * FOX_WIDTH].reshape(bs, dec, FOX_HEADS, HEAD_DIM))
        outs_s["fox_v"].append(zs[:, :dec, QB0 + 2 * FOX_WIDTH:QB0 + 3 * FOX_WIDTH].reshape(bs, dec, FOX_HEADS, HEAD_DIM))
        outs_s["logf"].append(lf_s[:, :dec, FB_LANE:FB_LANE + FOX_HEADS])
        outs_s["win_k"].append(nk_s.reshape(bs, wbuf, NSA_KV, HEAD_DIM))
        outs_s["win_v"].append(nv_s.reshape(bs, wbuf, NSA_KV, HEAD_DIM))

    st = lambda dct, k: jnp.stack(dct[k])
    return (xp, xs,
            st(outs_p, "nsa_k"), st(outs_p, "nsa_v"), st(outs_p, "fox_k"), st(outs_p, "fox_v"),
            st(outs_p, "logf"), st(outs_p, "win_k"), st(outs_p, "win_v"),
            st(outs_s, "nsa_k"), st(outs_s, "nsa_v"), st(outs_s, "fox_k"), st(outs_s, "fox_v"),
            st(outs_s, "logf"), st(outs_s, "win_k"), st(outs_s, "win_v"))
```

```python
import functools
import math

import numpy as np
import jax
import jax.numpy as jnp
from jax import lax
from jax.experimental import pallas as pl
from jax.experimental.pallas import tpu as pltpu

F32 = jnp.float32
BF = jnp.bfloat16

D_MODEL = 2048
DEPTH = 2
HEAD_DIM = 128
NSA_HEADS = D_MODEL // (2 * HEAD_DIM)
NSA_KV = max(1, NSA_HEADS // 4)
NSA_REP = NSA_HEADS // NSA_KV
FOX_HEADS = D_MODEL // (2 * HEAD_DIM)
NSA_WIDTH = NSA_HEADS * HEAD_DIM
FOX_WIDTH = FOX_HEADS * HEAD_DIM
CMP_LEN = 32
CMP_STRIDE = 16
CMP_HID = 256
SEL_BLOCK = 64
SEL_TOPN = 16
WINDOW = 512
REL_BUCKETS = 32
REL_MAX_DIST = 128
PEER_HEADS = 8
PEER_NKEYS = 128
PEER_EXPERTS = PEER_NKEYS * PEER_NKEYS
PEER_DK = 256
PEER_TOPK = 16
DEEPNORM_ALPHA = (2 * DEPTH) ** 0.25
LN_EPS = 1e-5
NEG_INF = -1e9
FORCE_SCORE = 1e9
SCALE = HEAD_DIM ** -0.5

LANES = 128
SUB = 8
VMEM_MB = 1024 * 1024

QA0 = 0
KV0 = QA0 + NSA_WIDTH
QB0 = KV0 + 6 * NSA_KV * HEAD_DIM
GM0 = QB0 + 3 * FOX_WIDTH
GF0 = GM0 + 2 * D_MODEL
PROJ_COLS = 10240
FB_LANE = 3 * NSA_HEADS
SAMPLE_ROWS = 8
PEER_TOK = 512
PEER_ETILE = 512
FLASH_T = 256
FOX_T = 512
FLASH_HPS = 2


def _nt(a, b):
    return lax.dot_general(a, b, (((1,), (1,)), ((), ())), preferred_element_type=F32)


def _tn(a, b):
    return lax.dot_general(a, b, (((0,), (0,)), ((), ())), preferred_element_type=F32)


def _dot(a, b):
    return jnp.dot(a, b, preferred_element_type=F32)


def _split3(x):
    hi = x.astype(BF)
    r1 = x - hi.astype(F32)
    mid = r1.astype(BF)
    lo = (r1 - mid.astype(F32)).astype(BF)
    return hi, mid, lo


def _params(sem, mb):
    return pltpu.CompilerParams(dimension_semantics=sem, vmem_limit_bytes=mb * VMEM_MB)


def _lane_col(x, idx):
    lane = lax.broadcasted_iota(jnp.int32, x.shape, 1)
    return jnp.sum(jnp.where(lane == idx, x, 0.0), axis=-1, keepdims=True)


def _ada_body(c_ref, w_ref, b_ref, o_ref):
    c = c_ref[...]
    a = (c * jax.nn.sigmoid(c)).astype(BF)
    o_ref[...] = _dot(a, w_ref[...].astype(BF)) + b_ref[...]


def _ada(c, w, b):
    m, d = c.shape
    n = w.shape[1]
    tn = 1536
    return pl.pallas_call(
        _ada_body,
        grid=(n // tn,),
        in_specs=[pl.BlockSpec((m, d), lambda j: (0, 0)),
                  pl.BlockSpec((d, tn), lambda j: (0, j)),
                  pl.BlockSpec((1, tn), lambda j: (0, j))],
        out_specs=pl.BlockSpec((m, tn), lambda j: (0, j)),
        out_shape=jax.ShapeDtypeStruct((m, n), F32),
        compiler_params=_params(("parallel",), 48),
        name="ada",
    )(c, w, b)


def _proj_body(x_ref, sh_ref, sc_ref, w_ref, o_ref, *rest, emit_h):
    if emit_h:
        hb_ref, h_ref = rest
    else:
        (h_ref,) = rest

    @pl.when(pl.program_id(2) == 0)
    def _():
        h = (x_ref[0] * (1.0 + sc_ref[0]) + sh_ref[0]).astype(BF)
        h_ref[...] = h
        if emit_h:
            hb_ref[0] = h

    o_ref[0] = _dot(h_ref[...], w_ref[...])


def _proj(x, shift, scale, w, tm, tn, emit_h=False):
    n, t, d = x.shape
    c = w.shape[1]
    ts = shift.shape[1]
    tms = tm if ts == t else 1
    mod_map = (lambda b, i, j: (b, i, 0)) if ts == t else (lambda b, i, j: (b, 0, 0))
    out_shape = [jax.ShapeDtypeStruct((n, t, c), F32)]
    out_specs = [pl.BlockSpec((1, tm, tn), lambda b, i, j: (b, i, j))]
    if emit_h:
        out_shape.append(jax.ShapeDtypeStruct((n, t, d), BF))
        out_specs.append(pl.BlockSpec((1, tm, d), lambda b, i, j: (b, i, 0)))
    res = pl.pallas_call(
        functools.partial(_proj_body, emit_h=emit_h),
        grid=(n, t // tm, c // tn),
        in_specs=[pl.BlockSpec((1, tm, d), lambda b, i, j: (b, i, 0)),
                  pl.BlockSpec((1, tms, d), mod_map),
                  pl.BlockSpec((1, tms, d), mod_map),
                  pl.BlockSpec((d, tn), lambda b, i, j: (0, j))],
        out_specs=out_specs,
        out_shape=out_shape,
        scratch_shapes=[pltpu.VMEM((tm, d), BF)],
        compiler_params=_params(("parallel", "parallel", "arbitrary"), 56),
        name="proj",
    )(x, shift, scale, w)
    return res if emit_h else res[0]


def _cumsum_rows(x, tri):
    hi, mid, lo = _split3(x)
    return _dot(tri, hi) + _dot(tri, mid) + _dot(tri, lo)


def _gates_body(z_ref, bf_ref, ga_ref, lf_ref, f_ref, *, t, blk):
    z = z_ref[0]
    ga_ref[0] = jax.nn.sigmoid(z)
    lf = jax.nn.log_sigmoid(z + bf_ref[...])
    lf_ref[0] = lf
    r = lax.broadcasted_iota(jnp.int32, (blk, blk), 0)
    c = lax.broadcasted_iota(jnp.int32, (blk, blk), 1)
    tri = (r >= c).astype(BF)
    carry = jnp.zeros((1, LANES), F32)
    for b in range(t // blk):
        fb = _cumsum_rows(lf[b * blk:(b + 1) * blk], tri) + carry
        f_ref[0, b * blk:(b + 1) * blk, :] = fb
        carry = fb[blk - 1:blk, :]


def _gates(z, bfpad):
    n, t, _ = z.shape
    blk = min(256, t)
    sds = jax.ShapeDtypeStruct((n, t, LANES), F32)
    spec = pl.BlockSpec((1, t, LANES), lambda b: (b, 0, 0))
    return pl.pallas_call(
        functools.partial(_gates_body, t=t, blk=blk),
        grid=(n,),
        in_specs=[pl.BlockSpec((1, t, LANES), lambda b: (b, 0, GF0 // LANES)),
                  pl.BlockSpec((1, LANES), lambda b: (0, 0))],
        out_specs=[spec, spec, spec],
        out_shape=[sds, sds, sds],
        compiler_params=_params(("parallel",), 32),
        name="gates",
    )(z, bfpad)


def _compress_core(xc, w1_ref, pos_ref, w2_ref, kind):
    m = xc.shape[0]
    half = CMP_STRIDE * HEAD_DIM
    first = _dot(xc, w1_ref[kind, 0:half, :])
    second = _dot(xc, w1_ref[kind, half:2 * half, :])
    posb = jnp.broadcast_to(pos_ref[kind], (SUB, 2 * half)).astype(BF)
    pos_term = _dot(posb, w1_ref[kind])[0:1]
    hid = jax.nn.gelu(first + pltpu.roll(second, m - 1, 0) + pos_term)
    return _dot(hid.astype(BF), w2_ref[kind])


def _compress_p_body(x_ref, w1_ref, pos_ref, w2_ref, o_ref, xc_ref, *, m):
    kind = pl.program_id(1) // NSA_KV
    for s in range(CMP_STRIDE):
        xc_ref[:, s * HEAD_DIM:(s + 1) * HEAD_DIM] = x_ref[0, pl.ds(s, m, stride=CMP_STRIDE), :].astype(BF)
    o_ref[0, 0] = _compress_core(xc_ref[...], w1_ref, pos_ref, w2_ref, kind)


def _compress_p(z, w1, pos, w2):
    n, t, _ = z.shape
    m = t // CMP_STRIDE
    return pl.pallas_call(
        functools.partial(_compress_p_body, m=m),
        grid=(n, 2 * NSA_KV),
        in_specs=[pl.BlockSpec((1, t, HEAD_DIM), lambda b, a: (b, 0, KV0 // HEAD_DIM + a)),
                  pl.BlockSpec(w1.shape, lambda b, a: (0, 0, 0)),
                  pl.BlockSpec(pos.shape, lambda b, a: (0, 0, 0)),
                  pl.BlockSpec(w2.shape, lambda b, a: (0, 0, 0))],
        out_specs=pl.BlockSpec((1, 1, m, HEAD_DIM), lambda b, a: (b, a, 0, 0)),
        out_shape=jax.ShapeDtypeStruct((n, 2 * NSA_KV, m, HEAD_DIM), F32),
        scratch_shapes=[pltpu.VMEM((m, CMP_STRIDE * HEAD_DIM), BF)],
        compiler_params=_params(("parallel", "parallel"), 32),
        name="compress_p",
    )(z, w1, pos, w2)


def _masked_softmax(s, mask):
    s = jnp.where(mask, s, NEG_INF)
    p = jnp.exp(s - jnp.max(s, axis=-1, keepdims=True)) * mask.astype(F32)
    return p / jnp.maximum(jnp.sum(p, axis=-1, keepdims=True), 1e-30)


def _cmp_sel_math(qs, kc, vc, bias, tpos, nb, ns, ovl, blocks_on_rows=False, tpos_row=None):
    tq = qs[0].shape[0]
    lane = lax.broadcasted_iota(jnp.int32, (tq, LANES), 1)
    dist = tpos - (lane * CMP_STRIDE + (CMP_LEN - 1))
    mask = (dist >= 0) & (lane < nb)
    kcb = kc.astype(BF)
    vcb = vc.astype(BF)
    outs = []
    psum = jnp.zeros((tq, LANES), F32)
    for r in range(NSA_REP):
        p = _masked_softmax(_nt(qs[r], kcb) * SCALE + bias[r], mask)
        outs.append(_dot(p.astype(BF), vcb))
        psum = psum + p
    hi, mid, lo = _split3(psum)
    if blocks_on_rows:
        nr = -(-ns // SUB) * SUB
        imp = (_nt(ovl, hi) + _nt(ovl, mid) + _nt(ovl, lo))[0:nr]
        blk = lax.broadcasted_iota(jnp.int32, (nr, tq), 0)
        trow = tpos_row
        cur = trow // SEL_BLOCK
        forced = (blk == 0) | (blk == cur) | (blk == cur - 1)
        score = jnp.where(forced, FORCE_SCORE, jnp.where(blk * SEL_BLOCK <= trow, imp, NEG_INF))
        score = jnp.where(blk < ns, score, -3.0e38)
        rank = jnp.zeros((nr, tq), F32)
        for j in range(ns):
            rj = score[j:j + 1, :]
            rank = rank + ((rj > score) | ((rj == score) & (blk > j))).astype(F32)
        sel = ((rank < float(min(SEL_TOPN, ns))) & (blk < ns)).astype(F32)
        return outs, jnp.concatenate([sel, jnp.zeros((LANES - nr, tq), F32)], axis=0)
    imp = _dot(hi, ovl) + _dot(mid, ovl) + _dot(lo, ovl)
    cur = tpos // SEL_BLOCK
    valid = lane * SEL_BLOCK <= tpos
    forced = (lane == 0) | (lane == cur) | (lane == cur - 1)
    score = jnp.where(forced, FORCE_SCORE, jnp.where(valid, imp, NEG_INF))
    score = jnp.where(lane < ns, score, -3.0e38)
    rank = jnp.zeros((tq, LANES), F32)
    for j in range(ns):
        cj = score[:, j:j + 1]
        ahead = (cj > score) | ((cj == score) & (lane > j))
        rank = rank + ahead.astype(F32)
    sel = (rank < float(min(SEL_TOPN, ns))) & (lane < ns)
    return outs, sel.astype(F32)


def _cmp_p_body(q_ref, kc_ref, vc_ref, bias_ref, ga_ref, ovl_ref, o_ref, sel_ref, *, tq, nb, ns):
    g = pl.program_id(1)
    qi = pl.program_id(2)
    q = q_ref[0]
    qs = [q[:, r * HEAD_DIM:(r + 1) * HEAD_DIM].astype(BF) for r in range(NSA_REP)]
    tpos = qi * tq + lax.broadcasted_iota(jnp.int32, (tq, 1), 0)
    tpos_row = qi * tq + lax.broadcasted_iota(jnp.int32, (1, tq), 1)
    outs, sel_t = _cmp_sel_math(qs, kc_ref[0, 0], vc_ref[0, 0], bias_ref[...], tpos, nb, ns, ovl_ref[...],
                                blocks_on_rows=True, tpos_row=tpos_row)
    ga = ga_ref[0]
    for r in range(NSA_REP):
        gate = _lane_col(ga, (g * NSA_REP + r) * 3 + 0)
        o_ref[0, :, r * HEAD_DIM:(r + 1) * HEAD_DIM] = outs[r] * gate
    sel_ref[0, 0] = sel_t


def _cmp_p(z, kvc, bias_cmp, ga, ovl, nb, ns):
    n, t, _ = z.shape
    tq = min(FLASH_T, t)
    gw = NSA_REP * HEAD_DIM
    m = kvc.shape[2]
    return pl.pallas_call(
        functools.partial(_cmp_p_body, tq=tq, nb=nb, ns=ns),
        grid=(n, NSA_KV, t // tq),
        in_specs=[pl.BlockSpec((1, tq, gw), lambda b, g, i: (b, i, g)),
                  pl.BlockSpec((1, 1, m, HEAD_DIM), lambda b, g, i: (b, g, 0, 0)),
                  pl.BlockSpec((1, 1, m, HEAD_DIM), lambda b, g, i: (b, NSA_KV + g, 0, 0)),
                  pl.BlockSpec((NSA_REP, tq, LANES), lambda b, g, i: (g, i, 0)),
                  pl.BlockSpec((1, tq, LANES), lambda b, g, i: (b, i, 0)),
                  pl.BlockSpec((LANES, LANES), lambda b, g, i: (0, 0))],
        out_specs=[pl.BlockSpec((1, tq, gw), lambda b, g, i: (b, i, g)),
                   pl.BlockSpec((1, 1, LANES, tq), lambda b, g, i: (b, g, 0, i))],
        out_shape=[jax.ShapeDtypeStruct((n, t, NSA_WIDTH), F32),
                   jax.ShapeDtypeStruct((n, NSA_KV, LANES, t), F32)],
        compiler_params=_params(("parallel", "parallel", "parallel"), 32),
        name="cmp_p",
    )(z, kvc, kvc, bias_cmp, ga, ovl)


def _flash_body(tab_ref, *refs, mode, tq, hps):
    if mode == "sel":
        q_ref, k_ref, v_ref, bias_ref, ga_ref, sel_ref, o_ref, m_ref, l_ref, acc_ref = refs
    elif mode == "win":
        q_ref, k_ref, v_ref, bias_ref, ga_ref, o_ref, m_ref, l_ref, acc_ref = refs
    else:
        q_ref, k_ref, v_ref, fcol_ref, frow_ref, o_ref, m_ref, l_ref, acc_ref = refs
    rep = 1 if mode == "fox" else NSA_REP
    w = rep * HEAD_DIM
    hg0 = pl.program_id(1) * hps
    pair = pl.program_id(2)
    qi = tab_ref[0, pair]
    tile = tab_ref[1, pair]

    @pl.when(tab_ref[2, pair] == 1)
    def _():
        m_ref[...] = jnp.full(m_ref.shape, NEG_INF, F32)
        l_ref[...] = jnp.zeros(l_ref.shape, F32)
        acc_ref[...] = jnp.zeros(acc_ref.shape, F32)

    def scores(hh):
        q = q_ref[0, :, hh * w:(hh + 1) * w]
        if rep == 1:
            qs = q.astype(BF)
        else:
            qs = jnp.concatenate([q[:, r * HEAD_DIM:(r + 1) * HEAD_DIM] for r in range(rep)], axis=0).astype(BF)
        return _nt(k_ref[0, :, hh * HEAD_DIM:(hh + 1) * HEAD_DIM].astype(BF), qs) * SCALE

    def causal():
        ki = lax.broadcasted_iota(jnp.int32, (tq, tq), 0)
        qj = lax.broadcasted_iota(jnp.int32, (tq, tq), 1)
        return (qi - tile) * tq + qj - ki

    def update(hh, s, mask):
        if mask is not None:
            s = jnp.where(mask, s, NEG_INF)
        m_old = m_ref[hh]
        m_new = jnp.maximum(m_old, jnp.max(s, axis=0, keepdims=True))
        p = jnp.exp(s - m_new)
        if mask is not None:
            p = p * mask.astype(F32)
        alpha = jnp.exp(m_old - m_new)
        l_ref[hh] = l_ref[hh] * alpha + jnp.sum(p, axis=0, keepdims=True)
        acc_ref[hh] = acc_ref[hh] * alpha + _tn(v_ref[0, :, hh * HEAD_DIM:(hh + 1) * HEAD_DIM].astype(BF), p.astype(BF))
        m_ref[hh] = m_new

    if mode == "fox":
        ss = []
        for hh in range(hps):
            fk = _lane_col(fcol_ref[0], FB_LANE + hg0 + hh)
            ss.append(scores(hh) + (frow_ref[0, pl.ds(hg0 + hh, 1), :] - fk))

        @pl.when(tile == qi)
        def _():
            mask = causal() >= 0
            for hh in range(hps):
                update(hh, ss[hh], mask)

        @pl.when(tile != qi)
        def _():
            for hh in range(hps):
                update(hh, ss[hh], None)
    elif mode == "win":
        ss = [scores(hh) + bias_ref[0, hh] for hh in range(hps)]
        inner = (tile < qi) & ((qi - tile + 1) * tq - 1 <= WINDOW)

        @pl.when(inner)
        def _():
            for hh in range(hps):
                update(hh, ss[hh], None)

        @pl.when(jnp.logical_not(inner))
        def _():
            dist = causal()
            mask = jnp.concatenate([(dist >= 0) & (dist <= WINDOW)] * rep, axis=1)
            for hh in range(hps):
                update(hh, ss[hh], mask)
    else:
        dist = causal()
        key_blk = (tile * tq + lax.broadcasted_iota(jnp.int32, (tq, LANES), 0)) // SEL_BLOCK
        expand = (key_blk == lax.broadcasted_iota(jnp.int32, (tq, LANES), 1)).astype(BF)
        for hh in range(hps):
            hit = _dot(expand, sel_ref[0, hh].astype(BF))
            mask = (dist >= 0) & (hit > 0.5)
            update(hh, scores(hh) + bias_ref[0, hh], jnp.concatenate([mask] * rep, axis=1))

    @pl.when(tab_ref[3, pair] == 1)
    def _():
        for hh in range(hps):
            o = acc_ref[hh] / jnp.maximum(l_ref[hh], 1e-30)
            if mode == "fox":
                o_ref[0, :, hh * HEAD_DIM:(hh + 1) * HEAD_DIM] = o.T
            else:
                ga = ga_ref[0]
                branch = 1 if mode == "sel" else 2
                for r in range(rep):
                    hd = (hg0 + hh) * NSA_REP + r
                    gate = _lane_col(ga, hd * 3 + branch)
                    o_ref[0, :, (hh * rep + r) * HEAD_DIM:(hh * rep + r + 1) * HEAD_DIM] = (
                        o[:, r * tq:(r + 1) * tq].T * gate)


def _flash_tile(mode, t):
    return min(FOX_T if mode == "fox" else FLASH_T, t)


def _flash(mode, z, aux):
    n, t, _ = z.shape
    tq = _flash_tile(mode, t)
    nq = t // tq
    back = WINDOW // tq if mode == "win" else nq
    pairs = [(i, k) for i in range(nq) for k in range(max(i - back, 0), i + 1)]
    tab = jnp.asarray(np.array(
        [[i for i, _ in pairs], [k for _, k in pairs],
         [int(k == max(i - back, 0)) for i, k in pairs], [int(k == i) for i, k in pairs]], np.int32))
    if mode == "fox":
        heads, rep = FOX_HEADS, 1
        qc, kc, vc = QB0 // HEAD_DIM, (QB0 + FOX_WIDTH) // HEAD_DIM, (QB0 + 2 * FOX_WIDTH) // HEAD_DIM
    else:
        heads, rep = NSA_KV, NSA_REP
        off = 2 if mode == "sel" else 4
        qc = QA0 // (rep * HEAD_DIM)
        kc = KV0 // HEAD_DIM + off * NSA_KV
        vc = KV0 // HEAD_DIM + (off + 1) * NSA_KV
    w = rep * HEAD_DIM
    hps = FLASH_HPS
    assert heads % hps == 0 and qc % hps == 0 and kc % hps == 0 and vc % hps == 0
    qc, kc, vc = qc // hps, kc // hps, vc // hps
    in_specs = [pl.BlockSpec((1, tq, hps * w), lambda b, h, p, tb: (b, tb[0, p], qc + h)),
                pl.BlockSpec((1, tq, hps * HEAD_DIM), lambda b, h, p, tb: (b, tb[1, p], kc + h)),
                pl.BlockSpec((1, tq, hps * HEAD_DIM), lambda b, h, p, tb: (b, tb[1, p], vc + h))]
    if mode == "fox":
        f, ft = aux
        args = (z, z, z, f, ft)
        in_specs += [pl.BlockSpec((1, tq, LANES), lambda b, h, p, tb: (b, tb[1, p], 0)),
                     pl.BlockSpec((1, FOX_HEADS, tq), lambda b, h, p, tb: (b, 0, tb[0, p]))]
    else:
        bias, ga = aux[0], aux[1]
        args = (z, z, z, bias, ga)
        in_specs += [pl.BlockSpec((1, hps, tq, rep * tq),
                                  lambda b, h, p, tb: (jnp.minimum(tb[0, p] - tb[1, p], 2), h, 0, 0)),
                     pl.BlockSpec((1, tq, LANES), lambda b, h, p, tb: (b, tb[0, p], 0))]
        if mode == "sel":
            args += (aux[2],)
            in_specs += [pl.BlockSpec((1, hps, LANES, tq), lambda b, h, p, tb: (b, h, 0, tb[0, p]))]
    gs = pltpu.PrefetchScalarGridSpec(
        num_scalar_prefetch=1, grid=(n, heads // hps, len(pairs)), in_specs=in_specs,
        out_specs=pl.BlockSpec((1, tq, hps * w), lambda b, h, p, tb: (b, tb[0, p], h)),
        scratch_shapes=[pltpu.VMEM((hps, 1, rep * tq), F32), pltpu.VMEM((hps, 1, rep * tq), F32),
                        pltpu.VMEM((hps, HEAD_DIM, rep * tq), F32)])
    return pl.pallas_call(
        functools.partial(_flash_body, mode=mode, tq=tq, hps=hps),
        grid_spec=gs,
        out_shape=jax.ShapeDtypeStruct((n, t, heads * w), F32),
        compiler_params=_params(("parallel", "parallel", "arbitrary"), 48),
        name="flash_" + mode,
    )(tab, *args)


def _merge1_body(oc_ref, os_ref, ow_ref, ob_ref, ga_ref, gb_ref, wa_ref, wb_ref, u_ref, oa_s, ob_s):
    @pl.when(pl.program_id(1) == 0)
    def _():
        oa_s[...] = (oc_ref[...] + os_ref[...] + ow_ref[...]).astype(BF)
        ob_s[...] = ob_ref[...].astype(BF)

    ya = _dot(oa_s[...], wa_ref[...])
    yb = _dot(ob_s[...], wb_ref[...])
    u_ref[...] = (jax.nn.sigmoid(ga_ref[...]) * ya + jax.nn.sigmoid(gb_ref[...]) * yb).astype(BF)


def _merge1(oc, osel, ow, ob, z2d, wa, wb):
    rows = oc.shape[0]
    tm, tn = 512, 512
    ospec = pl.BlockSpec((tm, NSA_WIDTH), lambda i, j: (i, 0))
    return pl.pallas_call(
        _merge1_body,
        grid=(rows // tm, D_MODEL // tn),
        in_specs=[ospec, ospec, ospec, ospec,
                  pl.BlockSpec((tm, tn), lambda i, j: (i, GM0 // tn + j)),
                  pl.BlockSpec((tm, tn), lambda i, j: (i, (GM0 + D_MODEL) // tn + j)),
                  pl.BlockSpec((NSA_WIDTH, tn), lambda i, j: (0, j)),
                  pl.BlockSpec((FOX_WIDTH, tn), lambda i, j: (0, j))],
        out_specs=pl.BlockSpec((tm, tn), lambda i, j: (i, j)),
        out_shape=jax.ShapeDtypeStruct((rows, D_MODEL), BF),
        scratch_shapes=[pltpu.VMEM((tm, NSA_WIDTH), BF), pltpu.VMEM((tm, FOX_WIDTH), BF)],
        compiler_params=_params(("parallel", "arbitrary"), 48),
        name="merge1",
    )(oc, osel, ow, ob, z2d, z2d, wa, wb)


def _layer_norm(y, g, b):
    mu = jnp.mean(y, axis=-1, keepdims=True)
    yc = y - mu
    var = jnp.mean(yc * yc, axis=-1, keepdims=True)
    return yc * lax.rsqrt(var + LN_EPS) * g + b


def _merge2_body(u_ref, wo_ref, x_ref, gt_ref, g_ref, b_ref, o_ref):
    mix = _dot(u_ref[0], wo_ref[...])
    y = DEEPNORM_ALPHA * x_ref[0] + gt_ref[0] * mix
    o_ref[0] = _layer_norm(y, g_ref[...], b_ref[...])


def _merge2(u, wo, x, gate, g, b):
    n, t, d = x.shape
    tm = min(512, t)
    ts = gate.shape[1]
    gspec = (pl.BlockSpec((1, tm, d), lambda bb, i: (bb, i, 0)) if ts == t
             else pl.BlockSpec((1, 1, d), lambda bb, i: (bb, 0, 0)))
    vec = pl.BlockSpec((1, d), lambda bb, i: (0, 0))
    return pl.pallas_call(
        _merge2_body,
        grid=(n, t // tm),
        in_specs=[pl.BlockSpec((1, tm, d), lambda bb, i: (bb, i, 0)),
                  pl.BlockSpec((d, d), lambda bb, i: (0, 0)),
                  pl.BlockSpec((1, tm, d), lambda bb, i: (bb, i, 0)),
                  gspec, vec, vec],
        out_specs=pl.BlockSpec((1, tm, d), lambda bb, i: (bb, i, 0)),
        out_shape=jax.ShapeDtypeStruct((n, t, d), F32),
        compiler_params=_params(("parallel", "parallel"), 56),
        name="merge2",
    )(u, wo, x, gate, g, b)


def _merge_desc(lst):
    n = len(lst)
    j = n // 2
    while j >= 1:
        for i in range(n):
            l = i ^ j
            if l > i:
                lst[i], lst[l] = jnp.maximum(lst[i], lst[l]), jnp.minimum(lst[i], lst[l])
        j //= 2
    return lst


def _sort_desc(lst):
    n = len(lst)
    k = 2
    while k <= n:
        j = k // 2
        while j >= 1:
            for i in range(n):
                l = i ^ j
                if l > i:
                    hi, lo = jnp.maximum(lst[i], lst[l]), jnp.minimum(lst[i], lst[l])
                    lst[i], lst[l] = (hi, lo) if (i & k) == 0 else (lo, hi)
            j //= 2
        k *= 2
    return lst


def _top_across_sublanes(lst):
    n = len(lst)
    for sh in (SUB // 2, SUB // 4, SUB // 8):
        part = [pltpu.roll(x, sh, 0) for x in lst]
        lst = _merge_desc([jnp.maximum(lst[i], part[n - 1 - i]) for i in range(n)])
    return lst


def _route_body(q_ref, sk_ref, a_ref, b_ref, ea_ref, eb_ref, tau_ref, *, tt):
    half = PEER_DK // 2
    k = PEER_TOPK
    groups = PEER_NKEYS // SUB

    def head(h, tau_all):
        r0 = pl.multiple_of(h * PEER_NKEYS, PEER_NKEYS)
        c0 = pl.multiple_of(h * PEER_DK, PEER_DK)
        a_ref[0, pl.ds(r0, PEER_NKEYS), :] = _nt(sk_ref[h, 0], q_ref[:, pl.ds(c0, half)].astype(BF))
        b_ref[0, pl.ds(r0, PEER_NKEYS), :] = _nt(sk_ref[h, 1], q_ref[:, pl.ds(c0 + half, half)].astype(BF))
        sub = lax.broadcasted_iota(jnp.int32, (SUB, LANES), 0)
        taus = []
        for c in range(tt // LANES):
            lanes = slice(c * LANES, (c + 1) * LANES)
            tops = []
            for ref in (a_ref, b_ref):
                rows = [ref[0, pl.ds(r0 + SUB * r, SUB), lanes] for r in range(groups)]
                tops.append(_top_across_sublanes(_sort_desc(rows)))
            t1, t2 = tops
            best = None
            for kk in range(k // SUB):
                spread = t2[SUB * kk]
                for s in range(1, SUB):
                    spread = jnp.where(sub == s, t2[SUB * kk + s], spread)
                cand = [t1[r] + spread for r in range(k)]
                best = cand if best is None else _merge_desc([jnp.maximum(best[i], cand[k - 1 - i]) for i in range(k)])
            tv = _top_across_sublanes(best)
            zsum = jnp.exp(tv[0] - tv[0])
            for i in range(1, k):
                zsum = zsum + jnp.exp(tv[i] - tv[0])
            s1 = a_ref[0, pl.ds(r0, PEER_NKEYS), lanes]
            s2 = b_ref[0, pl.ds(r0, PEER_NKEYS), lanes]
            ea_ref[0, pl.ds(r0, PEER_NKEYS), lanes] = jnp.exp(s1 - t1[0][0:1]) / zsum[0:1]
            eb_ref[0, pl.ds(r0, PEER_NKEYS), lanes] = jnp.exp(s2 - t2[0][0:1])
            taus.append(tv[k - 1])
        head_row = lax.broadcasted_iota(jnp.int32, (PEER_HEADS, tt), 0) == h
        return jnp.where(head_row, jnp.concatenate(taus, axis=1), tau_all)

    tau_ref[0] = lax.fori_loop(0, PEER_HEADS, head, jnp.zeros((PEER_HEADS, tt), F32))


def _route(q2d, sk):
    rows = q2d.shape[0]
    tt = PEER_TOK
    nblk = rows // tt
    big = jax.ShapeDtypeStruct((nblk, PEER_HEADS * PEER_NKEYS, tt), F32)
    bspec = pl.BlockSpec((1, PEER_HEADS * PEER_NKEYS, tt), lambda i: (i, 0, 0))
    return pl.pallas_call(
        functools.partial(_route_body, tt=tt),
        grid=(nblk,),
        in_specs=[pl.BlockSpec((tt, PEER_HEADS * PEER_DK), lambda i: (i, 0)),
                  pl.BlockSpec(sk.shape, lambda i: (0, 0, 0, 0))],
        out_specs=[bspec, bspec, bspec, bspec, pl.BlockSpec((1, PEER_HEADS, tt), lambda i: (i, 0, 0))],
        out_shape=[big, big, big, big, jax.ShapeDtypeStruct((nblk, PEER_HEADS, tt), F32)],
        compiler_params=_params(("parallel",), 48),
        name="route",
    )(q2d, sk)


def _peer_body(h_ref, u_ref, v_ref, a_ref, b_ref, ea_ref, eb_ref, tau_ref, x_ref, gt_ref, g_ref, bb_ref,
               o_ref, acc_ref, pre_ref, act_ref, *, te, tt):
    j = pl.program_id(1)
    nj = pl.num_programs(1) - 1
    slab = PEER_NKEYS
    cur = j % 2

    @pl.when(j == 0)
    def _():
        acc_ref[...] = jnp.zeros(acc_ref.shape, F32)
        act_ref[1] = jnp.zeros(act_ref.shape[1:], BF)

    tile = jnp.minimum(j, nj - 1)
    acc_ref[...] += _tn(act_ref[1 - cur], v_ref[...])
    pre_ref[...] = _nt(u_ref[...], h_ref[...])
    for sidx in range(te // slab):
        a = tile * (te // PEER_NKEYS) + (sidx * slab) // PEER_NKEYS
        b0 = (sidx * slab) % PEER_NKEYS
        gm = None
        for h in range(PEER_HEADS):
            arow = a_ref[0, pl.ds(h * PEER_NKEYS + a, 1), :]
            earow = ea_ref[0, pl.ds(h * PEER_NKEYS + a, 1), :]
            bs = b_ref[0, h * PEER_NKEYS + b0:h * PEER_NKEYS + b0 + slab, :]
            ebs = eb_ref[0, h * PEER_NKEYS + b0:h * PEER_NKEYS + b0 + slab, :]
            term = jnp.where(arow + bs >= tau_ref[0, h:h + 1, :], earow * ebs, 0.0)
            gm = term if gm is None else gm + term
        pre = pre_ref[sidx * slab:(sidx + 1) * slab, :]
        act_ref[cur, sidx * slab:(sidx + 1) * slab, :] = (jax.nn.gelu(pre) * gm).astype(BF)

    @pl.when(j == nj)
    def _():
        y = DEEPNORM_ALPHA * x_ref[...] + gt_ref[0] * acc_ref[...]
        o_ref[...] = _layer_norm(y, g_ref[...], bb_ref[...])


def _peer(h2, u, v, a, b, ea, eb, tau, x1, gate, g, bb, rows_per_gate):
    rows, d = x1.shape
    tt, te = PEER_TOK, PEER_ETILE
    nblk = rows // tt
    nj = PEER_EXPERTS // te
    one = pl.Buffered(1)
    rspec = pl.BlockSpec((1, PEER_HEADS * PEER_NKEYS, tt), lambda i, j: (i, 0, 0), pipeline_mode=one)
    if rows_per_gate == 1:
        gspec = pl.BlockSpec((1, tt, d), lambda i, j: (i, 0, 0), pipeline_mode=one)
    else:
        gspec = pl.BlockSpec((1, 1, d), lambda i, j: (i * tt // rows_per_gate, 0, 0))
    vec = pl.BlockSpec((1, d), lambda i, j: (0, 0))
    return pl.pallas_call(
        functools.partial(_peer_body, te=te, tt=tt),
        grid=(nblk, nj + 1),
        in_specs=[pl.BlockSpec((tt, d), lambda i, j: (i, 0), pipeline_mode=one),
                  pl.BlockSpec((te, d), lambda i, j: (jnp.minimum(j, nj - 1), 0)),
                  pl.BlockSpec((te, d), lambda i, j: (jnp.maximum(j - 1, 0), 0)),
                  rspec, rspec, rspec, rspec,
                  pl.BlockSpec((1, PEER_HEADS, tt), lambda i, j: (i, 0, 0)),
                  pl.BlockSpec((tt, d), lambda i, j: (i, 0), pipeline_mode=one),
                  gspec, vec, vec],
        out_specs=pl.BlockSpec((tt, d), lambda i, j: (i, 0)),
        out_shape=jax.ShapeDtypeStruct((rows, d), F32),
        scratch_shapes=[pltpu.VMEM((tt, d), F32), pltpu.VMEM((te, tt), F32), pltpu.VMEM((2, te, tt), BF)],
        compiler_params=_params(("parallel", "arbitrary"), 56),
        name="peer",
    )(h2, u, v, a, b, ea, eb, tau, x1, gate, g, bb)


def _pad_rows(x, rows):
    return jnp.concatenate([x, jnp.zeros((rows - x.shape[0], x.shape[1]), x.dtype)], axis=0)


def _nsa_s_body(pt_ref, *refs, npg):
    del pt_ref
    kp = refs[:npg]
    vp = refs[npg:2 * npg]
    (z_ref, ga_ref, w1_ref, pos_ref, w2_ref, bc_ref, bs_ref, ovl_ref, oc_ref, os_ref, xc_ref) = refs[2 * npg:]
    past = npg * LANES
    nb = past // CMP_STRIDE - 1
    ns = -(-(past + 4) // SEL_BLOCK)
    rows = SAMPLE_ROWS
    ga = ga_ref[0]
    tpos = past + lax.broadcasted_iota(jnp.int32, (rows, 1), 0)
    rowi = lax.broadcasted_iota(jnp.int32, (rows, LANES), 0)
    lane = lax.broadcasted_iota(jnp.int32, (rows, LANES), 1)
    cpp = LANES // CMP_STRIDE
    rstride = 2 * NSA_KV
    for g in range(NSA_KV):
        kvc = []
        for kind, pages in enumerate((kp, vp)):
            for p in range(npg):
                for s in range(CMP_STRIDE):
                    xc_ref[p * cpp:(p + 1) * cpp, s * HEAD_DIM:(s + 1) * HEAD_DIM] = (
                        pages[p][0, pl.ds(rstride * s + g, cpp, stride=rstride * CMP_STRIDE), :])
            kvc.append(_compress_core(xc_ref[...].astype(BF), w1_ref, pos_ref, w2_ref, kind))
        qs = [z_ref[0, :, QA0 + (g * NSA_REP + r) * HEAD_DIM:QA0 + (g * NSA_REP + r + 1) * HEAD_DIM].astype(BF)
              for r in range(NSA_REP)]
        outs, sel = _cmp_sel_math(qs, kvc[0], kvc[1], bc_ref[g * NSA_REP:(g + 1) * NSA_REP], tpos, nb, ns,
                                  ovl_ref[...])
        for r in range(NSA_REP):
            hd = g * NSA_REP + r
            oc_ref[0, :, hd * HEAD_DIM:(hd + 1) * HEAD_DIM] = outs[r] * ga[:, hd * 3:hd * 3 + 1]
        qst = jnp.concatenate(qs, axis=0)
        ks0 = KV0 + 2 * NSA_KV * HEAD_DIM + g * HEAD_DIM
        vs0 = KV0 + 3 * NSA_KV * HEAD_DIM + g * HEAD_DIM
        knew = _pad_rows(z_ref[0, :, ks0:ks0 + HEAD_DIM], LANES).astype(BF)
        vnew = _pad_rows(z_ref[0, :, vs0:vs0 + HEAD_DIM], LANES).astype(BF)
        bpp = LANES // SEL_BLOCK
        s_tiles, m_tiles = [], []
        for p in range(npg + 1):
            if p < npg:
                kt = kp[p][0, pl.ds(NSA_KV + g, LANES, stride=rstride), :].astype(BF)
                mk = jnp.zeros((rows, LANES), F32)
                for bq in range(bpp):
                    in_blk = (lane >= bq * SEL_BLOCK) & (lane < (bq + 1) * SEL_BLOCK)
                    mk = jnp.where(in_blk, sel[:, p * bpp + bq:p * bpp + bq + 1], mk)
                mk = mk > 0.5
            else:
                kt = knew
                mk = (lane <= rowi) & (lane < rows) & (sel[:, ns - 1:ns] > 0.5)
            bias = bs_ref[g * NSA_REP:(g + 1) * NSA_REP, :, p * LANES:(p + 1) * LANES].reshape(NSA_REP * rows, LANES)
            mk = jnp.concatenate([mk] * NSA_REP, axis=0)
            s_tiles.append(jnp.where(mk, _nt(qst, kt) * SCALE + bias, NEG_INF))
            m_tiles.append(mk)
        mx = s_tiles[0].max(axis=-1, keepdims=True)
        for st in s_tiles[1:]:
            mx = jnp.maximum(mx, st.max(axis=-1, keepdims=True))
        den = jnp.zeros((NSA_REP * rows, 1), F32)
        acc = jnp.zeros((NSA_REP * rows, HEAD_DIM), F32)
        for p in range(npg + 1):
            pr = jnp.exp(s_tiles[p] - mx) * m_tiles[p].astype(F32)
            den = den + jnp.sum(pr, axis=-1, keepdims=True)
            if p < npg:
                vt = vp[p][0, pl.ds(NSA_KV + g, LANES, stride=rstride), :].astype(BF)
            else:
                vt = vnew
            acc = acc + _dot(pr.astype(BF), vt)
        o = acc / jnp.maximum(den, 1e-30)
        for r in range(NSA_REP):
            hd = g * NSA_REP + r
            os_ref[0, :, hd * HEAD_DIM:(hd + 1) * HEAD_DIM] = o[r * rows:(r + 1) * rows] * ga[:, hd * 3 + 1:hd * 3 + 2]


def _nsa_s(pt, ck, cv, zs, ga, w1, pos, w2, bias_c, bias_s, ovl):
    bsz, npg = pt.shape
    rows = SAMPLE_ROWS
    prow = ck.shape[1]
    page = lambda p: pl.BlockSpec((1, prow, HEAD_DIM), functools.partial(lambda b, t, p: (t[b, p], 0, 0), p=p))
    const = lambda shp: pl.BlockSpec(shp, lambda b, t: (0,) * len(shp))
    in_specs = ([page(p) for p in range(npg)] + [page(p) for p in range(npg)]
                + [pl.BlockSpec((1, rows, PROJ_COLS), lambda b, t: (b, 0, 0)),
                   pl.BlockSpec((1, rows, LANES), lambda b, t: (b, 0, 0)),
                   const(w1.shape), const(pos.shape), const(w2.shape), const(bias_c.shape), const(bias_s.shape),
                   const(ovl.shape)])
    ospec = pl.BlockSpec((1, rows, NSA_WIDTH), lambda b, t: (b, 0, 0))
    gs = pltpu.PrefetchScalarGridSpec(
        num_scalar_prefetch=1, grid=(bsz,), in_specs=in_specs, out_specs=[ospec, ospec],
        scratch_shapes=[pltpu.VMEM((npg * LANES // CMP_STRIDE, CMP_STRIDE * HEAD_DIM), F32)])
    sds = jax.ShapeDtypeStruct((bsz, rows, NSA_WIDTH), F32)
    return pl.pallas_call(
        functools.partial(_nsa_s_body, npg=npg),
        grid_spec=gs, out_shape=[sds, sds],
        compiler_params=_params(("parallel",), 56),
        name="nsa_s",
    )(pt, *([ck] * npg), *([cv] * npg), zs, ga, w1, pos, w2, bias_c, bias_s, ovl)


def _win_s_body(sk_ref, sv_ref, z_ref, ga_ref, bw_ref, o_ref, nk_ref, nv_ref, *, wbuf, nnew):
    rows = SAMPLE_ROWS
    ga = ga_ref[0]
    rowi = lax.broadcasted_iota(jnp.int32, (rows, LANES), 0)
    lane = lax.broadcasted_iota(jnp.int32, (rows, LANES), 1)
    rowk = lax.broadcasted_iota(jnp.int32, (rows, wbuf), 0)
    colk = lax.broadcasted_iota(jnp.int32, (rows, wbuf), 1)
    dist_old = wbuf + rowk - colk
    m_old = jnp.concatenate([(dist_old >= 0) & (dist_old <= WINDOW)] * NSA_REP, axis=0)
    m_new = jnp.concatenate([(lane <= rowi) & (lane < rows) & (rowi - lane <= WINDOW)] * NSA_REP, axis=0)
    for g in range(NSA_KV):
        qst = jnp.concatenate(
            [z_ref[0, :, QA0 + (g * NSA_REP + r) * HEAD_DIM:QA0 + (g * NSA_REP + r + 1) * HEAD_DIM]
             for r in range(NSA_REP)], axis=0).astype(BF)
        kw0 = KV0 + 4 * NSA_KV * HEAD_DIM + g * HEAD_DIM
        vw0 = KV0 + 5 * NSA_KV * HEAD_DIM + g * HEAD_DIM
        kold = sk_ref[0, pl.ds(g, wbuf, stride=NSA_KV), :].astype(BF)
        vold = sv_ref[0, pl.ds(g, wbuf, stride=NSA_KV), :].astype(BF)
        knew = _pad_rows(z_ref[0, :, kw0:kw0 + HEAD_DIM], LANES).astype(BF)
        vnew = _pad_rows(z_ref[0, :, vw0:vw0 + HEAD_DIM], LANES).astype(BF)
        b_old = bw_ref[g * NSA_REP:(g + 1) * NSA_REP, :, 0:wbuf].reshape(NSA_REP * rows, wbuf)
        b_new = bw_ref[g * NSA_REP:(g + 1) * NSA_REP, :, wbuf:wbuf + LANES].reshape(NSA_REP * rows, LANES)
        s_old = jnp.where(m_old, _nt(qst, kold) * SCALE + b_old, NEG_INF)
        s_new = jnp.where(m_new, _nt(qst, knew) * SCALE + b_new, NEG_INF)
        mx = jnp.maximum(s_old.max(axis=-1, keepdims=True), s_new.max(axis=-1, keepdims=True))
        p_old = jnp.exp(s_old - mx) * m_old.astype(F32)
        p_new = jnp.exp(s_new - mx) * m_new.astype(F32)
        den = jnp.sum(p_old, axis=-1, keepdims=True) + jnp.sum(p_new, axis=-1, keepdims=True)
        o = (_dot(p_old.astype(BF), vold) + _dot(p_new.astype(BF), vnew)) / jnp.maximum(den, 1e-30)
        for r in range(NSA_REP):
            hd = g * NSA_REP + r
            o_ref[0, :, hd * HEAD_DIM:(hd + 1) * HEAD_DIM] = o[r * rows:(r + 1) * rows] * ga[:, hd * 3 + 2:hd * 3 + 3]
    keep = (wbuf - nnew) * NSA_KV
    nk_ref[0, 0:keep, :] = sk_ref[0, nnew * NSA_KV:wbuf * NSA_KV, :]
    nv_ref[0, 0:keep, :] = sv_ref[0, nnew * NSA_KV:wbuf * NSA_KV, :]
    for (c0, dst) in ((KV0 + 4 * NSA_KV * HEAD_DIM, nk_ref), (KV0 + 5 * NSA_KV * HEAD_DIM, nv_ref)):
        new = jnp.concatenate(
            [z_ref[0, i:i + 1, c0 + g * HEAD_DIM:c0 + (g + 1) * HEAD_DIM] for i in range(nnew) for g in range(NSA_KV)],
            axis=0)
        dst[0, keep:wbuf * NSA_KV, :] = new


def _win_s(st_k, st_v, layer, zs, ga, bias_w, nnew):
    bsz = zs.shape[0]
    rows = SAMPLE_ROWS
    srows = st_k.shape[1]
    wbuf = srows // NSA_KV
    sspec = pl.BlockSpec((1, srows, HEAD_DIM), lambda b: (layer * bsz + b, 0, 0))
    nspec = pl.BlockSpec((1, srows, HEAD_DIM), lambda b: (b, 0, 0))
    nsds = jax.ShapeDtypeStruct((bsz, srows, HEAD_DIM), F32)
    return pl.pallas_call(
        functools.partial(_win_s_body, wbuf=wbuf, nnew=nnew),
        grid=(bsz,),
        in_specs=[sspec, sspec,
                  pl.BlockSpec((1, rows, PROJ_COLS), lambda b: (b, 0, 0)),
                  pl.BlockSpec((1, rows, LANES), lambda b: (b, 0, 0)),
                  pl.BlockSpec(bias_w.shape, lambda b: (0, 0, 0))],
        out_specs=[pl.BlockSpec((1, rows, NSA_WIDTH), lambda b: (b, 0, 0)), nspec, nspec],
        out_shape=[jax.ShapeDtypeStruct((bsz, rows, NSA_WIDTH), F32), nsds, nsds],
        compiler_params=_params(("parallel",), 32),
        name="win_s",
    )(st_k, st_v, zs, ga, bias_w)


def _fox_s_body(pt_ref, *refs, npg):
    del pt_ref
    kp = refs[:npg]
    vp = refs[npg:2 * npg]
    fp = refs[2 * npg:3 * npg]
    z_ref, lf_ref, o_ref = refs[3 * npg:]
    rows = SAMPLE_ROWS
    rowi = lax.broadcasted_iota(jnp.int32, (rows, LANES), 0)
    lane = lax.broadcasted_iota(jnp.int32, (rows, LANES), 1)
    r128 = lax.broadcasted_iota(jnp.int32, (LANES, LANES), 0)
    c128 = lax.broadcasted_iota(jnp.int32, (LANES, LANES), 1)
    upper = (r128 <= c128).astype(BF)
    ft = []
    carry = jnp.zeros((LANES, 1), F32)
    for p in range(npg):
        lp = fp[p][0]
        lpad = jnp.concatenate([lp, jnp.zeros((LANES, LANES - FOX_HEADS), F32)], axis=1)
        hi, mid, lo = _split3(lpad)
        cs = _tn(hi, upper) + _tn(mid, upper) + _tn(lo, upper) + carry
        carry = cs[:, LANES - 1:LANES]
        ft.append(cs[0:FOX_HEADS])
    lf_new = lf_ref[0]
    x = lf_new
    for sh in (1, 2, 4):
        x = x + jnp.where(rowi >= sh, pltpu.roll(x, sh, 0), 0.0)
    tot_t = jnp.broadcast_to(carry, (LANES, LANES)).T
    tot_row = pltpu.roll(tot_t[0:1, :], FB_LANE, 1)
    f_new = x + tot_row
    f_new_t = _pad_rows(f_new, LANES).T
    m_new = (lane <= rowi) & (lane < rows)
    for h in range(FOX_HEADS):
        q = z_ref[0, :, QB0 + h * HEAD_DIM:QB0 + (h + 1) * HEAD_DIM].astype(BF)
        k0 = QB0 + FOX_WIDTH + h * HEAD_DIM
        v0 = QB0 + 2 * FOX_WIDTH + h * HEAD_DIM
        knew = _pad_rows(z_ref[0, :, k0:k0 + HEAD_DIM], LANES).astype(BF)
        vnew = _pad_rows(z_ref[0, :, v0:v0 + HEAD_DIM], LANES).astype(BF)
        fq = f_new[:, FB_LANE + h:FB_LANE + h + 1]
        s_tiles = []
        for p in range(npg):
            kt = kp[p][0, pl.ds(h, LANES, stride=FOX_HEADS), :].astype(BF)
            s_tiles.append(_nt(q, kt) * SCALE + (fq - ft[p][h:h + 1, :]))
        s_n = _nt(q, knew) * SCALE + (fq - f_new_t[FB_LANE + h:FB_LANE + h + 1, :])
        s_tiles.append(jnp.where(m_new, s_n, NEG_INF))
        mx = s_tiles[0].max(axis=-1, keepdims=True)
        for st in s_tiles[1:]:
            mx = jnp.maximum(mx, st.max(axis=-1, keepdims=True))
        den = jnp.zeros((rows, 1), F32)
        acc = jnp.zeros((rows, HEAD_DIM), F32)
        for p in range(npg + 1):
            pr = jnp.exp(s_tiles[p] - mx)
            if p == npg:
                pr = pr * m_new.astype(F32)
                vt = vnew
            else:
                vt = vp[p][0, pl.ds(h, LANES, stride=FOX_HEADS), :].astype(BF)
            den = den + jnp.sum(pr, axis=-1, keepdims=True)
            acc = acc + _dot(pr.astype(BF), vt)
        o_ref[0, :, h * HEAD_DIM:(h + 1) * HEAD_DIM] = acc / jnp.maximum(den, 1e-30)


def _fox_s(pt, ck, cv, cf, zs, lf):
    bsz, npg = pt.shape
    rows = SAMPLE_ROWS
    prow = ck.shape[1]
    pmap = lambda p: functools.partial(lambda b, t, p: (t[b, p], 0, 0), p=p)
    in_specs = ([pl.BlockSpec((1, prow, HEAD_DIM), pmap(p)) for p in range(npg)]
                + [pl.BlockSpec((1, prow, HEAD_DIM), pmap(p)) for p in range(npg)]
                + [pl.BlockSpec((1, LANES, FOX_HEADS), pmap(p)) for p in range(npg)]
                + [pl.BlockSpec((1, rows, PROJ_COLS), lambda b, t: (b, 0, 0)),
                   pl.BlockSpec((1, rows, LANES), lambda b, t: (b, 0, 0))])
    gs = pltpu.PrefetchScalarGridSpec(
        num_scalar_prefetch=1, grid=(bsz,), in_specs=in_specs,
        out_specs=pl.BlockSpec((1, rows, FOX_WIDTH), lambda b, t: (b, 0, 0)))
    return pl.pallas_call(
        functools.partial(_fox_s_body, npg=npg),
        grid_spec=gs, out_shape=jax.ShapeDtypeStruct((bsz, rows, FOX_WIDTH), F32),
        compiler_params=_params(("parallel",), 56),
        name="fox_s",
    )(pt, *([ck] * npg), *([cv] * npg), *([cf] * npg), zs, lf)


def _rel_bucket(dist):
    n = jnp.maximum(dist, 0)
    exact = REL_BUCKETS // 2
    scaled = jnp.log(jnp.maximum(n, 1).astype(F32) / exact) / math.log(REL_MAX_DIST / exact)
    large = jnp.minimum(exact + (scaled * (REL_BUCKETS - exact)).astype(jnp.int32), REL_BUCKETS - 1)
    return jnp.where(n < exact, n, large)


def _bias_of(rel_table, dist):
    bucket = _rel_bucket(dist)
    out = jnp.zeros((rel_table.shape[1],) + dist.shape, F32)
    for b in range(REL_BUCKETS):
        row = rel_table[b].reshape((-1,) + (1,) * dist.ndim)
        out = jnp.where(bucket[None] == b, row, out)
    return out


def _overlap(nb, ns):
    i = np.arange(LANES)[:, None]
    j = np.arange(LANES)[None, :]
    rs = SEL_BLOCK // CMP_STRIDE
    rc = CMP_LEN // CMP_STRIDE
    m = np.zeros((LANES, LANES), np.float32)
    for a in range(rs):
        for b in range(rc):
            m = m + (i == j * rs + a + b - (rc - 1))
    m = m * (i < nb) * (j < ns)
    return jnp.asarray(m, BF)


def kernel(x_prompt, x_sample, cache_nsa_k, cache_nsa_v, cache_fox_k, cache_fox_v, cache_fox_logf,
           state_nsa_win_k, state_nsa_win_v, page_table, c_prompt, c_sample, w_ada, b_ada, w_in,
           b_forget, cmp_pos, cmp_w1, cmp_w2, rel_table, w_branch_a, w_branch_b, w_out, ln_g, ln_b,
           peer_wq, peer_subkeys, peer_u, peer_v):
    bp, seq, d = x_prompt.shape
    bs, dec = x_sample.shape[:2]
    depth, n_pool, page = cache_nsa_k.shape[:3]
    npg = page_table.shape[1]
    past = npg * page
    wbuf = state_nsa_win_k.shape[2]
    rows = SAMPLE_ROWS
    assert d == D_MODEL and page == LANES and depth == DEPTH and dec <= rows
    assert seq % FLASH_T == 0 or seq < FLASH_T

    ck_nsa = cache_nsa_k.reshape(depth * n_pool, page * 2 * NSA_KV, HEAD_DIM)
    cv_nsa = cache_nsa_v.reshape(depth * n_pool, page * 2 * NSA_KV, HEAD_DIM)
    ck_fox = cache_fox_k.reshape(depth * n_pool, page * FOX_HEADS, HEAD_DIM)
    cv_fox = cache_fox_v.reshape(depth * n_pool, page * FOX_HEADS, HEAD_DIM)
    cf_fox = cache_fox_logf.reshape(depth * n_pool, page, FOX_HEADS)
    st_k = state_nsa_win_k.reshape(depth * bs, wbuf * NSA_KV, HEAD_DIM)
    st_v = state_nsa_win_v.reshape(depth * bs, wbuf * NSA_KV, HEAD_DIM)

    tq = _flash_tile("sel", seq)
    ii = jnp.arange(tq)

    def tile_bias(dd):
        b = _bias_of(rel_table, dd * tq + ii[None, :] - ii[:, None])
        return b.reshape(NSA_KV, NSA_REP, tq, tq).transpose(0, 2, 1, 3).reshape(NSA_KV, tq, NSA_REP * tq)

    bias_tile = jnp.stack([tile_bias(dd) for dd in range(3)])
    nb_p = seq // CMP_STRIDE - 1
    ns_p = -(-seq // SEL_BLOCK)
    lane = jnp.arange(LANES)
    bias_cmp_p = _bias_of(rel_table, jnp.arange(seq)[:, None] - (lane * CMP_STRIDE + CMP_LEN - 1)[None, :])
    nb_s = past // CMP_STRIDE - 1
    ns_s = -(-(past + dec) // SEL_BLOCK)
    tq_s = past + jnp.arange(rows)
    bias_cmp_s = _bias_of(rel_table, tq_s[:, None] - (lane * CMP_STRIDE + CMP_LEN - 1)[None, :])
    bias_sel_s = _bias_of(rel_table, tq_s[:, None] - jnp.arange(past + LANES)[None, :])
    bias_win_s = _bias_of(rel_table, wbuf + jnp.arange(rows)[:, None] - jnp.arange(wbuf + LANES)[None, :])
    ovl_p = _overlap(nb_p, ns_p).T
    ovl_s = _overlap(nb_s, ns_s)

    c_all = jnp.concatenate([c_prompt, c_sample], axis=0)
    xp = x_prompt
    xs = x_sample
    outs_p = {k: [] for k in ("nsa_k", "nsa_v", "fox_k", "fox_v", "logf", "win_k", "win_v")}
    outs_s = {k: [] for k in ("nsa_k", "nsa_v", "fox_k", "fox_v", "logf", "win_k", "win_v")}
    gh = NSA_KV * HEAD_DIM

    for l in range(depth):
        wl = w_in[l]
        o_ga = NSA_WIDTH + 6 * gh
        o_qb = o_ga + 3 * NSA_HEADS
        o_fb = o_qb + 3 * FOX_WIDTH
        o_gm = o_fb + FOX_HEADS
        wcat = jnp.concatenate(
            [wl[:, :o_ga], wl[:, o_qb:o_fb], wl[:, o_gm:], wl[:, o_ga:o_qb], wl[:, o_fb:o_gm],
             jnp.zeros((d, PROJ_COLS - GF0 - 3 * NSA_HEADS - FOX_HEADS), F32)], axis=1).astype(BF)
        bfpad = jnp.zeros((1, LANES), F32).at[0, FB_LANE:FB_LANE + FOX_HEADS].set(b_forget[l])
        w1 = cmp_w1[l].astype(BF)
        pos = cmp_pos[l].reshape(2, 1, CMP_LEN * HEAD_DIM)
        w2 = cmp_w2[l].astype(BF)
        wa = w_branch_a[l].astype(BF)
        wb = w_branch_b[l].astype(BF)
        wo = w_out[l].astype(BF)
        wq = peer_wq[l].astype(BF)
        sk = peer_subkeys[l].astype(BF)
        pu = peer_u[l].astype(BF)
        pv = peer_v[l].astype(BF)
        g1, b1 = ln_g[l, 0][None], ln_b[l, 0][None]
        g2, b2 = ln_g[l, 1][None], ln_b[l, 1][None]
        pt_l = page_table + l * n_pool

        mod = _ada(c_all, w_ada[l], b_ada[l][None]).reshape(bp + bs, 6, d)
        mod_p = mod[:bp]
        mod_s = mod[bp:]

        z = _proj(xp, mod_p[:, 0:1], mod_p[:, 1:2], wcat, tm=min(1024, seq), tn=1024)
        ga, lf, fcum = _gates(z, bfpad)
        kvc = _compress_p(z, w1, pos, w2)
        o_cmp, sel = _cmp_p(z, kvc, bias_cmp_p, ga, ovl_p, nb_p, ns_p)
        o_sel = _flash("sel", z, (bias_tile, ga, sel))
        o_win = _flash("win", z, (bias_tile, ga))
        f_t = jnp.swapaxes(fcum[:, :, FB_LANE:FB_LANE + FOX_HEADS], 1, 2)
        o_b = _flash("fox", z, (fcum, f_t))
        nrow = bp * seq
        u = _merge1(o_cmp.reshape(nrow, -1), o_sel.reshape(nrow, -1), o_win.reshape(nrow, -1),
                    o_b.reshape(nrow, -1), z.reshape(nrow, -1), wa, wb)
        x1 = _merge2(u.reshape(bp, seq, d), wo, xp, mod_p[:, 2:3], g1, b1)
        q, h2 = _proj(x1, mod_p[:, 3:4], mod_p[:, 4:5], wq, tm=min(1024, seq), tn=1024, emit_h=True)
        ra, rb, rea, reb, tau = _route(q.reshape(nrow, -1), sk)
        xp = _peer(h2.reshape(nrow, d), pu, pv, ra, rb, rea, reb, tau, x1.reshape(nrow, d),
                   mod_p[:, 5:6], g2, b2, rows_per_gate=seq).reshape(bp, seq, d)
        keep = min(WINDOW, seq)

        def piece(i0, r0=0):
            return z[:, r0:, KV0 + i0 * gh:KV0 + (i0 + 1) * gh].reshape(bp, seq - r0, NSA_KV, HEAD_DIM)

        outs_p["nsa_k"].append(jnp.stack([piece(0), piece(2)], axis=2))
        outs_p["nsa_v"].append(jnp.stack([piece(1), piece(3)], axis=2))
        outs_p["fox_k"].append(z[:, :, QB0 + FOX_WIDTH:QB0 + 2 * FOX_WIDTH].reshape(bp, seq, FOX_HEADS, HEAD_DIM))
        outs_p["fox_v"].append(z[:, :, QB0 + 2 * FOX_WIDTH:QB0 + 3 * FOX_WIDTH].reshape(bp, seq, FOX_HEADS, HEAD_DIM))
        outs_p["logf"].append(lf[:, :, FB_LANE:FB_LANE + FOX_HEADS])
        outs_p["win_k"].append(piece(4, seq - keep))
        outs_p["win_v"].append(piece(5, seq - keep))

        xs8 = jnp.pad(xs, ((0, 0), (0, rows - dec), (0, 0)))
        mrow = lambda k: jnp.repeat(mod_s[:, k], rows, axis=0)[None]
        zs = _proj(xs8.reshape(1, bs * rows, d), mrow(0), mrow(1), wcat, tm=min(512, bs * rows), tn=1024)
        ga_s, lf_s, _ = _gates(zs, bfpad)
        zs = zs.reshape(bs, rows, PROJ_COLS)
        ga_s = ga_s.reshape(bs, rows, LANES)
        lf_s = lf_s.reshape(bs, rows, LANES)
        oc_s, os_s = _nsa_s(pt_l, ck_nsa, cv_nsa, zs, ga_s, w1, pos, w2, bias_cmp_s, bias_sel_s, ovl_s)
        ow_s, nk_s, nv_s = _win_s(st_k, st_v, l, zs, ga_s, bias_win_s, dec)
        ob_s = _fox_s(pt_l, ck_fox, cv_fox, cf_fox, zs, lf_s)
        srow = bs * rows
        u_s = _merge1(oc_s.reshape(srow, -1), os_s.reshape(srow, -1), ow_s.reshape(srow, -1),
                      ob_s.reshape(srow, -1), zs.reshape(srow, -1), wa, wb)
        x1_s = _merge2(u_s.reshape(1, srow, d), wo, xs8.reshape(1, srow, d), mrow(2), g1, b1)
        x1_c = x1_s.reshape(bs, rows, d)[:, :dec].reshape(1, bs * dec, d)
        crow = lambda k: jnp.repeat(mod_s[:, k], dec, axis=0)[None]
        q_s, h2_s = _proj(x1_c, crow(3), crow(4), wq, tm=min(512, bs * dec), tn=1024, emit_h=True)
        ra, rb, rea, reb, tau = _route(q_s.reshape(bs * dec, -1), sk)
        xs = _peer(h2_s.reshape(bs * dec, d), pu, pv, ra, rb, rea, reb, tau, x1_c.reshape(bs * dec, d),
                   crow(5).reshape(-1, PEER_TOK, d), g2, b2, rows_per_gate=1).reshape(bs, dec, d)
        kv_s = zs[:, :dec, KV0:KV0 + 6 * gh].reshape(bs, dec, 6, NSA_KV, HEAD_DIM)
        outs_s["nsa_k"].append(kv_s[:, :, 0:4:2])
        outs_s["nsa_v"].append(kv_s[:, :, 1:4:2])
        outs_s["fox_k"].append(zs[:, :dec, QB0 + FOX_WIDTH:QB0 + 2 * FOX_WIDTH].reshape(bs, dec, FOX_HEADS, HEAD_DIM))
        outs_s["fox_v"].append(zs[:, :dec, QB0 + 2 * FOX_WIDTH:QB0 + 3 * FOX_WIDTH].reshape(bs, dec, FOX_HEADS, HEAD_DIM))
        outs_s["logf"].append(lf_s[:, :dec, FB_LANE:FB_LANE + FOX_HEADS])
        outs_s["win_k"].append(nk_s.reshape(bs, wbuf, NSA_KV, HEAD_DIM))
        outs_s["win_v"].append(nv_s.reshape(bs, wbuf, NSA_KV, HEAD_DIM))

    st = lambda dct, k: jnp.stack(dct[k])
    return (xp, xs,
            st(outs_p, "nsa_k"), st(outs_p, "nsa_v"), st(outs_p, "fox_k"), st(outs_p, "fox_v"),
            st(outs_p, "logf"), st(outs_p, "win_k"), st(outs_p, "win_v"),
            st(outs_s, "nsa_k"), st(outs_s, "nsa_v"), st(outs_s, "fox_k"), st(outs_s, "fox_v"),
            st(outs_s, "logf"), st(outs_s, "win_k"), st(outs_s, "win_v"))
```
